```python
import math
import jax
import jax.numpy as jnp
from jax import lax
import numpy as np

D_MODEL = 2048
BATCH = 2
SEQ = 8192
DEPTH = 2

F32 = jnp.float32
GRID_W = 64
CTX_LEN = 256
EPS = 1e-6

SSD_HEADS = 32
SSD_HEAD_DIM = 64
SSD_WIDTH = SSD_HEADS * SSD_HEAD_DIM
SSD_GROUPS = 4
SSD_HEADS_PER_GROUP = SSD_HEADS // SSD_GROUPS
SSD_STATE = 128
SSD_CHUNK = 128
D_CONV = 3
XBC_WIDTH = SSD_WIDTH + 2 * SSD_GROUPS * SSD_STATE

GM_GROUPS = 16
GM_GROUP_DIM = 128
GM_WIDTH = GM_GROUPS * GM_GROUP_DIM
GM_CHUNK = 128

HYB_CUTS = (SSD_WIDTH, SSD_WIDTH + XBC_WIDTH, SSD_WIDTH + XBC_WIDTH + 2 * SSD_HEADS,
            SSD_WIDTH + XBC_WIDTH + 2 * SSD_HEADS + GM_WIDTH)
HYB_IN = SSD_WIDTH + XBC_WIDTH + 2 * SSD_HEADS + 2 * GM_WIDTH
HYB_MIX = SSD_WIDTH + GM_WIDTH

MLA_HEADS = 16
Q_LORA = 768
KV_LORA = 512
QK_NOPE = 128
QK_ROPE = 64
V_DIM = 128
ROPE_PAIRS = QK_ROPE // 4
ROPE_THETA = 10000.0
MLA_IN = Q_LORA + KV_LORA + QK_ROPE
MLA_SCALE = (QK_NOPE + QK_ROPE) ** -0.5
Q_BLOCK = 128

N_EXPERTS = 32
TOP_K = 4
EXPERT_FF = D_MODEL
SWIGLU_LIMIT = 7.0
SWIGLU_ALPHA = 1.702
MOE_BLOCK = 128

N_EVEN = (DEPTH + 1) // 2
N_ODD = DEPTH // 2

kernel_name = 'hybrid_ssd_gmlp_mla_moe_prefix_dit'


def rms_norm(x, g):
    xf = x.astype(F32)
    y = xf * lax.rsqrt(jnp.mean(xf * xf, axis=-1, keepdims=True) + EPS)
    return y.astype(x.dtype) * g


def centred_dwconv(x, w, b):
    k = w.shape[0]
    pad = (k - 1) // 2
    n = x.shape[1]
    xp = jnp.pad(x, ((0, 0), (pad, k - 1 - pad), (0, 0)))
    out = b
    for i in range(k):
        out = out + xp[:, i:i + n] * w[i]
    return out


def axial_rope(rows, dtype):
    row = jnp.broadcast_to(jnp.arange(rows)[:, None], (rows, GRID_W)).reshape(-1)
    col = jnp.broadcast_to(jnp.arange(GRID_W)[None, :], (rows, GRID_W)).reshape(-1)
    freqs = ROPE_THETA ** (-jnp.arange(ROPE_PAIRS, dtype=F32) / ROPE_PAIRS)
    ang = jnp.stack([row[:, None] * freqs, col[:, None] * freqs], axis=1)
    return jnp.cos(ang).astype(dtype), jnp.sin(ang).astype(dtype)


def apply_rope(x, cos, sin):
    xr = x.reshape(*x.shape[:-1], 2, 2, ROPE_PAIRS)
    x1, x2 = xr[..., 0, :], xr[..., 1, :]
    c, s = cos[:, None], sin[:, None]
    out = jnp.stack([x1 * c - x2 * s, x1 * s + x2 * c], axis=-2)
    return out.reshape(x.shape)


def ssd_chunked(xh, dt, a_neg, bm, cm, h0):
    bsz, n = xh.shape[:2]
    nc = n // SSD_CHUNK
    G, J, P, N = SSD_GROUPS, SSD_HEADS_PER_GROUP, SSD_HEAD_DIM, SSD_STATE
    xc = xh.reshape(bsz, nc, SSD_CHUNK, G, J, P)
    dtc = dt.reshape(bsz, nc, SSD_CHUNK, G, J)
    bc = bm.reshape(bsz, nc, SSD_CHUNK, G, N)
    cc = cm.reshape(bsz, nc, SSD_CHUNK, G, N)
    a_cs = jnp.cumsum(dtc * a_neg, axis=2)
    xdt = xc * dtc[..., None]
    tril = jnp.tril(jnp.ones((SSD_CHUNK, SSD_CHUNK), dtype=bool))[:, :, None, None]
    seg = a_cs[:, :, :, None] - a_cs[:, :, None, :]
    lmat = jnp.exp(jnp.where(tril, seg, -jnp.inf))
    cb = jnp.einsum('bcqgn,bckgn->bcqkg', cc, bc)
    y_diag = jnp.einsum('bcqkgj,bckgjp->bcqgjp', cb[..., None] * lmat, xdt)
    decay_end = jnp.exp(a_cs[:, :, -1:] - a_cs)
    states = jnp.einsum('bckgn,bckgj,bckgjp->bcgjpn', bc, decay_end, xdt)
    chunk_decay = jnp.exp(a_cs[:, :, -1])

    def step(h, inp):
        s, dec = inp
        return h * dec[..., None, None] + s, h

    h_final, h_start = lax.scan(step, h0, (jnp.moveaxis(states, 1, 0), jnp.moveaxis(chunk_decay, 1, 0)))
    h_start = jnp.moveaxis(h_start, 0, 1)
    y_off = jnp.einsum('bcqgn,bcgjpn,bcqgj->bcqgjp', cc, h_start, jnp.exp(a_cs))
    return (y_diag + y_off).reshape(bsz, n, G, J, P), h_final


def chunk_gmlp(u, v, v_norm_g, w_s, b_s):
    bsz, n, _ = u.shape
    u = jax.nn.gelu(u)
    v = rms_norm(jax.nn.gelu(v), v_norm_g)
    vc = v.reshape(bsz, n // GM_CHUNK, GM_CHUNK, GM_GROUPS, GM_GROUP_DIM)
    s = jnp.einsum('gqk,bckgd->bcqgd', w_s, vc) + b_s.T[:, :, None]
    return u * s.reshape(bsz, n, GM_WIDTH)


def hybrid_mixer(h_ctx, h_lat, w_in, conv_w, conv_b, dt_bias, a_log, d_skip, ssd_norm_g,
                 v_norm_g, w_s, b_s, w_out, need_ctx):
    G, J = SSD_GROUPS, SSD_HEADS_PER_GROUP
    dt_b = dt_bias.astype(F32).reshape(2, G, J)
    a_neg = -jnp.exp(a_log.astype(F32)).reshape(2, G, J)

    def project(h):
        bsz, n, _ = h.shape
        z, xbc, dt_raw, u, v = jnp.split(h @ w_in, HYB_CUTS, axis=-1)
        xbc = jax.nn.silu(centred_dwconv(xbc, conv_w, conv_b))
        xs, bm, cm = jnp.split(xbc, [SSD_WIDTH, SSD_WIDTH + G * SSD_STATE], axis=-1)
        xs = xs.reshape(bsz, n, G, J, SSD_HEAD_DIM)
        bm = bm.reshape(bsz, n, G, SSD_STATE)
        cm = cm.reshape(bsz, n, G, SSD_STATE)
        dt = jax.nn.softplus(dt_raw.astype(F32).reshape(bsz, n, 2, G, J) + dt_b)
        return z, xs, bm, cm, dt, u, v

    zc, xc, bc, cc, dtc, uc, vc = project(h_ctx)
    zl, xl, bl, cl, dtl, ul, vl = project(h_lat)
    flip = lambda t: jnp.flip(t, axis=1)
    h0 = jnp.zeros((h_lat.shape[0], G, J, SSD_HEAD_DIM, SSD_STATE), F32)
    yc_f, hc_f = ssd_chunked(xc, dtc[:, :, 0], a_neg[0], bc, cc, h0)
    yl_f, _ = ssd_chunked(xl, dtl[:, :, 0], a_neg[0], bl, cl, hc_f)
    yc_b, hc_b = ssd_chunked(flip(xc), flip(dtc[:, :, 1]), a_neg[1], flip(bc), flip(cc), h0)
    yl_b, _ = ssd_chunked(flip(xl), flip(dtl[:, :, 1]), a_neg[1], flip(bl), flip(cl), hc_b)
    d = d_skip.reshape(G, J)[:, :, None]

    def merge(xs, yf, yb, z, u, v):
        y = (yf + flip(yb) + xs * d).reshape(z.shape).astype(z.dtype)
        y_ssd = rms_norm(y * jax.nn.silu(z), ssd_norm_g)
        y_gm = chunk_gmlp(u, v, v_norm_g, w_s, b_s)
        return jnp.concatenate([y_ssd, y_gm], axis=-1) @ w_out

    y_lat = merge(xl, yl_f, yl_b, zl, ul, vl)
    y_ctx = merge(xc, yc_f, yc_b, zc, uc, vc) if need_ctx else None
    return y_ctx, y_lat


def attend(q_nope, q_pe, k_nope, k_pe, v):
    s = jnp.einsum('bqhd,bkhd->bhqk', q_nope, k_nope) + jnp.einsum('bqhr,bkr->bhqk', q_pe, k_pe)
    p = jax.nn.softmax(s.astype(F32) * MLA_SCALE, axis=-1).astype(v.dtype)
    return jnp.einsum('bhqk,bkhd->bqhd', p, v)


def mla_mixer(h_ctx, h_lat, w_in, q_norm_g, kv_norm_g, w_uq, w_ukv, w_o, cos, sin, need_ctx):
    bsz, n, _ = h_lat.shape

    def queries(qp):
        q = (rms_norm(qp, q_norm_g) @ w_uq).reshape(*qp.shape[:2], MLA_HEADS, QK_NOPE + QK_ROPE)
        return q[..., :QK_NOPE], q[..., QK_NOPE:]

    def keys_values(kvp):
        c_kv, k_pe = jnp.split(kvp, [KV_LORA], axis=-1)
        kv = (rms_norm(c_kv, kv_norm_g) @ w_ukv).reshape(*kvp.shape[:2], MLA_HEADS, QK_NOPE + V_DIM)
        return kv[..., :QK_NOPE], k_pe, kv[..., QK_NOPE:]

    p_lat = h_lat @ w_in
    qn_l, qp_l = queries(p_lat[..., :Q_LORA])
    kn_l, kp_l, v_l = keys_values(p_lat[..., Q_LORA:])
    qp_l = apply_rope(qp_l, cos, sin)
    kp_l = apply_rope(kp_l[:, :, None], cos, sin)[:, :, 0]
    if need_ctx:
        p_ctx = h_ctx @ w_in
        kvp_ctx = p_ctx[..., Q_LORA:]
    else:
        kvp_ctx = h_ctx @ w_in[:, Q_LORA:]
    kn_c, kp_c, v_c = keys_values(kvp_ctx)
    kn_all = jnp.concatenate([kn_c, kn_l], axis=1)
    kp_all = jnp.concatenate([kp_c, kp_l], axis=1)
    v_all = jnp.concatenate([v_c, v_l], axis=1)
    nblk = n // Q_BLOCK
    qn_b = jnp.moveaxis(qn_l.reshape(bsz, nblk, Q_BLOCK, MLA_HEADS, QK_NOPE), 1, 0)
    qp_b = jnp.moveaxis(qp_l.reshape(bsz, nblk, Q_BLOCK, MLA_HEADS, QK_ROPE), 1, 0)
    o = lax.map(lambda qb: attend(qb[0], qb[1], kn_all, kp_all, v_all), (qn_b, qp_b))
    y_lat = jnp.moveaxis(o, 0, 1).reshape(bsz, n, MLA_HEADS * V_DIM) @ w_o
    y_ctx = None
    if need_ctx:
        qn_c, qp_c = queries(p_ctx[..., :Q_LORA])
        y_ctx = attend(qn_c, qp_c, kn_c, kp_c, v_c).reshape(bsz, h_ctx.shape[1], MLA_HEADS * V_DIM) @ w_o
    return y_ctx, y_lat


def moe(h, router_w, router_b, w1, b1, w2, b2):
    t_tok, d = h.shape
    logits = (h @ router_w + router_b).astype(F32)
    top_val, top_idx = lax.top_k(logits, TOP_K)
    gates = jax.nn.softmax(top_val, axis=-1).astype(h.dtype)
    flat_e = top_idx.reshape(-1)
    flat_tok = jnp.repeat(jnp.arange(t_tok, dtype=jnp.int32), TOP_K)
    flat_gate = gates.reshape(-1)
    order = jnp.argsort(flat_e)
    e_sorted, tok_sorted, gate_sorted = flat_e[order], flat_tok[order], flat_gate[order]
    counts = jnp.bincount(flat_e, length=N_EXPERTS)
    padded = (counts + MOE_BLOCK - 1) // MOE_BLOCK * MOE_BLOCK
    cum_pad = jnp.cumsum(padded)
    pad_start = cum_pad - padded
    start = jnp.cumsum(counts) - counts
    dest = pad_start[e_sorted] + jnp.arange(t_tok * TOP_K) - start[e_sorted]
    n_blocks = -(-(t_tok * TOP_K) // MOE_BLOCK) + N_EXPERTS
    n_slots = n_blocks * MOE_BLOCK
    slot_tok = jnp.full((n_slots,), t_tok, jnp.int32).at[dest].set(tok_sorted)
    slot_gate = jnp.zeros((n_slots,), h.dtype).at[dest].set(gate_sorted)
    block_expert = jnp.clip(jnp.searchsorted(cum_pad, jnp.arange(n_blocks) * MOE_BLOCK, side='right'),
                            0, N_EXPERTS - 1)
    h_pad = jnp.concatenate([h, jnp.zeros((1, d), h.dtype)], axis=0)
    xb = h_pad[slot_tok].reshape(n_blocks, MOE_BLOCK, d)

    def expert_block(args):
        xblk, e = args
        a = xblk @ w1[e] + b1[e]
        glu = jnp.minimum(a[:, 0::2], SWIGLU_LIMIT)
        lin = jnp.clip(a[:, 1::2], -SWIGLU_LIMIT, SWIGLU_LIMIT)
        y = glu * jax.nn.sigmoid(SWIGLU_ALPHA * glu) * (lin + 1.0)
        return y @ w2[e] + b2[e]

    yb = lax.map(expert_block, (xb, block_expert)).reshape(n_slots, d) * slot_gate[:, None]
    return jnp.zeros((t_tok + 1, d), h.dtype).at[slot_tok].add(yb)[:t_tok]


def setup_inputs(seed: int = 0) -> dict:
    key = jax.random.key(seed)
    ks = iter(jax.random.split(key, 48))
    D = D_MODEL

    def nrm(shape, scale):
        return jax.random.normal(next(ks), shape, F32) * scale

    def gain(shape):
        return 1.0 + nrm(shape, 0.02)

    x = nrm((BATCH, SEQ, D), 1.0)
    c = nrm((BATCH, D), 1.0)
    ctx = nrm((BATCH, CTX_LEN, D), 1.0)
    c_ctx = nrm((D,), 1.0)
    mix_norm_g = gain((DEPTH, D))
    ffn_norm_g = gain((DEPTH, D))
    ada_w = nrm((DEPTH, D, 6 * D), 0.5 * D ** -0.5)
    ada_b = nrm((DEPTH, 6 * D), 0.02)
    hyb_w_in = nrm((N_EVEN, D, HYB_IN), D ** -0.5)
    hyb_conv_w = nrm((N_EVEN, D_CONV, XBC_WIDTH), D_CONV ** -0.5)
    hyb_conv_b = nrm((N_EVEN, XBC_WIDTH), 0.02)
    dt0 = jnp.exp(jax.random.uniform(next(ks), (N_EVEN, 2, SSD_HEADS), F32, math.log(1e-3), math.log(1e-1)))
    hyb_dt_bias = dt0 + jnp.log(-jnp.expm1(-dt0))
    hyb_a_log = jnp.log(jax.random.uniform(next(ks), (N_EVEN, 2, SSD_HEADS), F32, 1.0, 16.0))
    hyb_d_skip = gain((N_EVEN, SSD_HEADS))
    hyb_ssd_norm_g = gain((N_EVEN, SSD_WIDTH))
    hyb_v_norm_g = gain((N_EVEN, GM_WIDTH))
    hyb_w_s = nrm((N_EVEN, GM_GROUPS, GM_CHUNK, GM_CHUNK), GM_CHUNK ** -0.5)
    hyb_b_s = gain((N_EVEN, GM_GROUPS, GM_CHUNK))
    hyb_w_out = nrm((N_EVEN, HYB_MIX, D), HYB_MIX ** -0.5)
    mla_w_in = nrm((N_ODD, D, MLA_IN), D ** -0.5)
    mla_q_norm_g = gain((N_ODD, Q_LORA))
    mla_kv_norm_g = gain((N_ODD, KV_LORA))
    mla_w_uq = nrm((N_ODD, Q_LORA, MLA_HEADS * (QK_NOPE + QK_ROPE)), Q_LORA ** -0.5)
    mla_w_ukv = nrm((N_ODD, KV_LORA, MLA_HEADS * (QK_NOPE + V_DIM)), KV_LORA ** -0.5)
    mla_w_o = nrm((N_ODD, MLA_HEADS * V_DIM, D), (MLA_HEADS * V_DIM) ** -0.5)
    router_w = nrm((DEPTH, D, N_EXPERTS), D ** -0.5)
    router_b = nrm((DEPTH, N_EXPERTS), 0.01)
    exp_w1 = nrm((DEPTH, N_EXPERTS, D, 2 * EXPERT_FF), D ** -0.5)
    exp_b1 = nrm((DEPTH, N_EXPERTS, 2 * EXPERT_FF), 0.02)
    exp_w2 = nrm((DEPTH, N_EXPERTS, EXPERT_FF, D), EXPERT_FF ** -0.5)
    exp_b2 = nrm((DEPTH, N_EXPERTS, D), 0.02)
    final_norm_g = gain((D,))
    return {'x': x, 'c': c, 'ctx': ctx, 'c_ctx': c_ctx,
            'mix_norm_g': mix_norm_g, 'ffn_norm_g': ffn_norm_g, 'ada_w': ada_w, 'ada_b': ada_b,
            'hyb_w_in': hyb_w_in, 'hyb_conv_w': hyb_conv_w, 'hyb_conv_b': hyb_conv_b,
            'hyb_dt_bias': hyb_dt_bias, 'hyb_a_log': hyb_a_log, 'hyb_d_skip': hyb_d_skip,
            'hyb_ssd_norm_g': hyb_ssd_norm_g, 'hyb_v_norm_g': hyb_v_norm_g,
            'hyb_w_s': hyb_w_s, 'hyb_b_s': hyb_b_s, 'hyb_w_out': hyb_w_out,
            'mla_w_in': mla_w_in, 'mla_q_norm_g': mla_q_norm_g, 'mla_kv_norm_g': mla_kv_norm_g,
            'mla_w_uq': mla_w_uq, 'mla_w_ukv': mla_w_ukv, 'mla_w_o': mla_w_o,
            'router_w': router_w, 'router_b': router_b,
            'exp_w1': exp_w1, 'exp_b1': exp_b1, 'exp_w2': exp_w2, 'exp_b2': exp_b2,
            'final_norm_g': final_norm_g}


def reference(x, c, ctx, c_ctx, mix_norm_g, ffn_norm_g, ada_w, ada_b,
              hyb_w_in, hyb_conv_w, hyb_conv_b, hyb_dt_bias, hyb_a_log, hyb_d_skip,
              hyb_ssd_norm_g, hyb_v_norm_g, hyb_w_s, hyb_b_s, hyb_w_out,
              mla_w_in, mla_q_norm_g, mla_kv_norm_g, mla_w_uq, mla_w_ukv, mla_w_o,
              router_w, router_b, exp_w1, exp_b1, exp_w2, exp_b2, final_norm_g):
    bsz, n, d = x.shape
    rows = n // GRID_W
    cos, sin = axial_rope(rows, x.dtype)
    silu_c = jax.nn.silu(c)
    silu_cc = jax.nn.silu(c_ctx)
    lat, cx = x, ctx
    n_ctx_tok = bsz * ctx.shape[1]
    for i in range(DEPTH):
        last = i == DEPTH - 1
        j = i // 2
        mod_l = (silu_c @ ada_w[i] + ada_b[i])[:, None, :]
        sh1, sc1, g1, sh2, sc2, g2 = jnp.split(mod_l, 6, axis=-1)
        n_mod_c = 2 if last else 6
        mods_c = jnp.split(silu_cc @ ada_w[i][:, :n_mod_c * d] + ada_b[i][:n_mod_c * d], n_mod_c)
        h_lat = rms_norm(lat, mix_norm_g[i]) * (1.0 + sc1) + sh1
        h_ctx = rms_norm(cx, mix_norm_g[i]) * (1.0 + mods_c[1]) + mods_c[0]
        if i % 2 == 0:
            y_ctx, y_lat = hybrid_mixer(h_ctx, h_lat, hyb_w_in[j], hyb_conv_w[j], hyb_conv_b[j],
                                        hyb_dt_bias[j], hyb_a_log[j], hyb_d_skip[j], hyb_ssd_norm_g[j],
                                        hyb_v_norm_g[j], hyb_w_s[j], hyb_b_s[j], hyb_w_out[j],
                                        need_ctx=not last)
        else:
            y_ctx, y_lat = mla_mixer(h_ctx, h_lat, mla_w_in[j], mla_q_norm_g[j], mla_kv_norm_g[j],
                                     mla_w_uq[j], mla_w_ukv[j], mla_w_o[j], cos, sin,
                                     need_ctx=not last)
        lat = lat + g1 * y_lat
        hf_lat = rms_norm(lat, ffn_norm_g[i]) * (1.0 + sc2) + sh2
        if last:
            f = moe(hf_lat.reshape(-1, d), router_w[i], router_b[i], exp_w1[i], exp_b1[i], exp_w2[i], exp_b2[i])
            lat = lat + g2 * f.reshape(bsz, n, d)
        else:
            cx = cx + mods_c[2] * y_ctx
            hf_ctx = rms_norm(cx, ffn_norm_g[i]) * (1.0 + mods_c[4]) + mods_c[3]
            tokens = jnp.concatenate([hf_ctx.reshape(-1, d), hf_lat.reshape(-1, d)], axis=0)
            f = moe(tokens, router_w[i], router_b[i], exp_w1[i], exp_b1[i], exp_w2[i], exp_b2[i])
            cx = cx + mods_c[5] * f[:n_ctx_tok].reshape(cx.shape)
            lat = lat + g2 * f[n_ctx_tok:].reshape(bsz, n, d)
    return rms_norm(lat, final_norm_g)
```

```python
import functools
import math

import jax
import jax.numpy as jnp
from jax import lax
from jax.experimental import pallas as pl
from jax.experimental.pallas import tpu as pltpu

F32 = jnp.float32
BF16 = jnp.bfloat16
HIGHEST = lax.Precision.HIGHEST
EPS = 1e-6

GRID_W = 64
SSD_HEADS = 32
SSD_HEAD_DIM = 64
SSD_WIDTH = SSD_HEADS * SSD_HEAD_DIM
SSD_GROUPS = 4
SSD_HEADS_PER_GROUP = SSD_HEADS // SSD_GROUPS
SSD_STATE = 128
SSD_CHUNK = 128
SSD_GROUP_WIDTH = SSD_HEADS_PER_GROUP * SSD_HEAD_DIM
BC_WIDTH = SSD_GROUPS * SSD_STATE
XBC_WIDTH = SSD_WIDTH + 2 * BC_WIDTH
GM_GROUPS = 16
GM_GROUP_DIM = 128
GM_WIDTH = GM_GROUPS * GM_GROUP_DIM
GM_CHUNK = 128
MLA_HEADS = 16
Q_LORA = 768
KV_LORA = 512
QK_NOPE = 128
QK_ROPE = 64
V_DIM = 128
ROPE_PAIRS = QK_ROPE // 4
ROPE_THETA = 10000.0
MLA_SCALE = (QK_NOPE + QK_ROPE) ** -0.5
TOP_K = 4
SWIGLU_LIMIT = 7.0
SWIGLU_ALPHA = 1.702
MOE_ROWS = 256

LANE = 128
VMEM_LIMIT = 56 * 1024 * 1024


def _pick(n, candidates):
    for c in candidates:
        if n % c == 0:
            return c
    raise ValueError(f"no tile for {n} in {candidates}")


def _cparams(sem):
    return pltpu.CompilerParams(dimension_semantics=sem, vmem_limit_bytes=VMEM_LIMIT)


def _is_ctx_rows(i, tm, rows_per_batch, n_ctx):
    row = (i * tm) % rows_per_batch + lax.broadcasted_iota(jnp.int32, (tm, 1), 0)
    return row < n_ctx


def _mod_select(mod_ref, idx, is_ctx):
    return jnp.where(is_ctx, mod_ref[0, 0, idx:idx + 1, :], mod_ref[0, 1, idx:idx + 1, :])


def _norm_rows(x, g):
    y = x * lax.rsqrt(jnp.mean(x * x, axis=-1, keepdims=True) + EPS)
    return y * g


def _fused_matmul_kernel(*refs, prologue, epilogue, tm, tn, rows_per_batch, n_ctx,
                         sh_idx, sc_idx, gate_idx, use_scratch):
    it = iter(refs)
    x_ref = next(it)
    g_ref = next(it) if prologue != "none" else None
    pmod_ref = next(it) if prologue == "norm_mod" else None
    w_ref = next(it)
    bias_ref = next(it) if epilogue == "bias" else None
    resid_ref = next(it) if epilogue == "gated_resid" else None
    emod_ref = next(it) if epilogue == "gated_resid" else None
    tab_ref = next(it) if epilogue == "mul_table" else None
    o_ref = next(it)
    xn_ref = next(it) if use_scratch else None

    i = pl.program_id(0)
    j = pl.program_id(1)

    if use_scratch:
        @pl.when(j == 0)
        def _():
            x = x_ref[...].astype(F32)
            if prologue != "none":
                x = _norm_rows(x, g_ref[...])
            if prologue == "norm_mod":
                is_ctx = _is_ctx_rows(i, tm, rows_per_batch, n_ctx)
                x = x * (1.0 + _mod_select(pmod_ref, sc_idx, is_ctx)) + _mod_select(pmod_ref, sh_idx, is_ctx)
            xn_ref[...] = x.astype(BF16)
        xn = xn_ref[...]
    else:
        xn = x_ref[...]

    acc = jnp.dot(xn, w_ref[...].astype(BF16), preferred_element_type=F32)
    if epilogue == "bias":
        acc = acc + bias_ref[...]
    elif epilogue == "gated_resid":
        is_ctx = _is_ctx_rows(i, tm, rows_per_batch, n_ctx)
        acc = resid_ref[...] + _mod_select(emod_ref, gate_idx, is_ctx) * acc
    elif epilogue == "mul_table":
        acc = acc * jnp.tile(tab_ref[...], (1, tn // LANE))
    o_ref[...] = acc.astype(o_ref.dtype)


def _fused_matmul(x, w, *, name, m_rows, k, tm, tn, out_dtype, rows_per_batch, n_ctx=0,
                  prologue="none", g=None, mod=None, sh_idx=0, sc_idx=0,
                  epilogue="none", bias=None, resid=None, gate_idx=0, table=None,
                  x_row_map=None, x_col_block=0, resid_row_map=None, table_row_map=None):
    n = w.shape[1]
    assert w.shape[0] == k and m_rows % tm == 0 and n % tn == 0 and rows_per_batch % tm == 0
    ident = lambda i: i
    x_row_map = x_row_map or ident
    resid_row_map = resid_row_map or ident
    table_row_map = table_row_map or ident
    batch_of = lambda i: (i * tm) // rows_per_batch
    use_scratch = prologue != "none" or x.dtype != BF16

    args = [x]
    in_specs = [pl.BlockSpec((tm, k), lambda i, j: (x_row_map(i), x_col_block))]
    if prologue != "none":
        args.append(g.reshape(1, k))
        in_specs.append(pl.BlockSpec((1, k), lambda i, j: (0, 0)))
    if prologue == "norm_mod":
        args.append(mod)
        in_specs.append(pl.BlockSpec((1, 2, 6, k), lambda i, j: (batch_of(i), 0, 0, 0)))
    args.append(w)
    in_specs.append(pl.BlockSpec((k, tn), lambda i, j: (0, j)))
    if epilogue == "bias":
        args.append(bias.reshape(1, n))
        in_specs.append(pl.BlockSpec((1, tn), lambda i, j: (0, j)))
    if epilogue == "gated_resid":
        args.append(resid)
        in_specs.append(pl.BlockSpec((tm, tn), lambda i, j: (resid_row_map(i), j)))
        args.append(mod)
        in_specs.append(pl.BlockSpec((1, 2, 6, tn), lambda i, j: (batch_of(i), 0, 0, j)))
    if epilogue == "mul_table":
        args.append(table)
        in_specs.append(pl.BlockSpec((tm, LANE), lambda i, j: (table_row_map(i), 0)))

    kern = functools.partial(
        _fused_matmul_kernel, prologue=prologue, epilogue=epilogue, tm=tm, tn=tn,
        rows_per_batch=rows_per_batch, n_ctx=n_ctx, sh_idx=sh_idx, sc_idx=sc_idx,
        gate_idx=gate_idx, use_scratch=use_scratch)
    return pl.pallas_call(
        kern,
        grid=(m_rows // tm, n // tn),
        in_specs=in_specs,
        out_specs=pl.BlockSpec((tm, tn), lambda i, j: (i, j)),
        out_shape=jax.ShapeDtypeStruct((m_rows, n), out_dtype),
        scratch_shapes=[pltpu.VMEM((tm, k), BF16)] if use_scratch else [],
        compiler_params=_cparams(("arbitrary", "arbitrary")),
        name=name,
    )(*args)


def _expand_heads(mat, col0):
    rows = mat.shape[0]
    return jnp.concatenate(
        [jnp.broadcast_to(mat[:, col0 + jj:col0 + jj + 1], (rows, SSD_HEAD_DIM))
         for jj in range(SSD_HEADS_PER_GROUP)], axis=1)


def _ssd_kernel(xs_ref, b_ref, c_ref, dtr_ref, dtb_ref, alog_ref, y_ref, h_ref, *, rev):
    L = SSD_CHUNK
    step = pl.program_id(1)

    @pl.when(step == 0)
    def _():
        h_ref[...] = jnp.zeros_like(h_ref)

    dcol = SSD_HEADS if rev else 0
    dt = jax.nn.softplus(dtr_ref[0] + dtb_ref[...])
    a = dt * (-jnp.exp(alog_ref[...]))
    r_i = lax.broadcasted_iota(jnp.int32, (L, L), 0)
    c_i = lax.broadcasted_iota(jnp.int32, (L, L), 1)
    tri = (r_i <= c_i) if rev else (r_i >= c_i)
    acs = jnp.dot(tri.astype(F32), a, precision=HIGHEST, preferred_element_type=F32)
    acs_t = acs.T
    edge = acs[0:1, :] if rev else acs[L - 1:L, :]
    decay_end = jnp.exp(edge - acs)
    exp_acs = jnp.exp(acs)
    chunk_decay = jnp.exp(edge)

    for g in range(SSD_GROUPS):
        col0 = dcol + g * SSD_HEADS_PER_GROUP
        bg = b_ref[0, :, g * SSD_STATE:(g + 1) * SSD_STATE]
        cg = c_ref[0, :, g * SSD_STATE:(g + 1) * SSD_STATE].astype(BF16)
        cb = lax.dot_general(cg, bg.astype(BF16), (((1,), (1,)), ((), ())), preferred_element_type=F32)
        xdt = xs_ref[0, :, g * SSD_GROUP_WIDTH:(g + 1) * SSD_GROUP_WIDTH] * _expand_heads(dt, col0)
        y_diag = []
        for jj in range(SSD_HEADS_PER_GROUP):
            col = col0 + jj
            seg = acs[:, col:col + 1] - acs_t[col:col + 1, :]
            lmat = jnp.exp(jnp.where(tri, seg, -jnp.inf))
            y_diag.append(jnp.dot((cb * lmat).astype(BF16),
                                  xdt[:, jj * SSD_HEAD_DIM:(jj + 1) * SSD_HEAD_DIM].astype(BF16),
                                  preferred_element_type=F32))
        y_diag = jnp.concatenate(y_diag, axis=1)
        h_g = h_ref[g]
        y_off = jnp.dot(cg, h_g.astype(BF16), preferred_element_type=F32) * _expand_heads(exp_acs, col0)
        y_ref[0, :, g * SSD_GROUP_WIDTH:(g + 1) * SSD_GROUP_WIDTH] = y_diag + y_off
        states = jnp.dot(bg.T.astype(BF16), (xdt * _expand_heads(decay_end, col0)).astype(BF16),
                         preferred_element_type=F32)
        h_ref[g] = h_g * _expand_heads(chunk_decay, col0) + states


def _ssd_scan(xbc, dt_raw, dt_bias, a_log, *, rev, n_ctx_chunks):
    bsz, nt, _ = xbc.shape
    nc = nt // SSD_CHUNK
    if rev:
        cmap = lambda s: jnp.where(s < n_ctx_chunks, n_ctx_chunks - 1 - s, nc - 1 + n_ctx_chunks - s)
    else:
        cmap = lambda s: s
    nb_x = SSD_WIDTH // BC_WIDTH
    return pl.pallas_call(
        functools.partial(_ssd_kernel, rev=rev),
        grid=(bsz, nc),
        in_specs=[
            pl.BlockSpec((1, SSD_CHUNK, SSD_WIDTH), lambda b, s: (b, cmap(s), 0)),
            pl.BlockSpec((1, SSD_CHUNK, BC_WIDTH), lambda b, s: (b, cmap(s), nb_x)),
            pl.BlockSpec((1, SSD_CHUNK, BC_WIDTH), lambda b, s: (b, cmap(s), nb_x + 1)),
            pl.BlockSpec((1, SSD_CHUNK, LANE), lambda b, s: (b, cmap(s), 0)),
            pl.BlockSpec((1, LANE), lambda b, s: (0, 0)),
            pl.BlockSpec((1, LANE), lambda b, s: (0, 0)),
        ],
        out_specs=pl.BlockSpec((1, SSD_CHUNK, SSD_WIDTH), lambda b, s: (b, cmap(s), 0)),
        out_shape=jax.ShapeDtypeStruct((bsz, nt, SSD_WIDTH), F32),
        scratch_shapes=[pltpu.VMEM((SSD_GROUPS, SSD_STATE, SSD_GROUP_WIDTH), F32)],
        compiler_params=_cparams(("arbitrary", "arbitrary")),
        name="ssd_scan_bwd" if rev else "ssd_scan_fwd",
    )(xbc, xbc, xbc, dt_raw, dt_bias, a_log)


def _mixer_merge_kernel(yf_ref, yb_ref, xs_ref, z_ref, u_ref, v_ref, dsk_ref, sg_ref, vg_ref,
                        ws_ref, bst_ref, o_ref):
    y = yf_ref[...] + yb_ref[...] + xs_ref[...] * dsk_ref[...]
    y_ssd = _norm_rows(y * jax.nn.silu(z_ref[...]), sg_ref[...])
    o_ref[:, :SSD_WIDTH] = y_ssd.astype(o_ref.dtype)
    u = jax.nn.gelu(u_ref[...])
    v = _norm_rows(jax.nn.gelu(v_ref[...]), vg_ref[...]).astype(BF16)
    for g in range(GM_GROUPS):
        sl = slice(g * GM_GROUP_DIM, (g + 1) * GM_GROUP_DIM)
        s = jnp.dot(ws_ref[g], v[:, sl], preferred_element_type=F32) + bst_ref[:, g:g + 1]
        o_ref[:, SSD_WIDTH + g * GM_GROUP_DIM:SSD_WIDTH + (g + 1) * GM_GROUP_DIM] = (u[:, sl] * s).astype(o_ref.dtype)


def _mixer_merge(yf, yb, xbc, proj, d_exp, ssd_g, v_g, w_s, b_s_t):
    rows = yf.shape[0]
    tm = GM_CHUNK
    row = lambda c: (lambda i: (i, c))
    const2 = lambda i: (0, 0)
    return pl.pallas_call(
        _mixer_merge_kernel,
        grid=(rows // tm,),
        in_specs=[
            pl.BlockSpec((tm, SSD_WIDTH), row(0)),
            pl.BlockSpec((tm, SSD_WIDTH), row(0)),
            pl.BlockSpec((tm, SSD_WIDTH), row(0)),
            pl.BlockSpec((tm, SSD_WIDTH), row(0)),
            pl.BlockSpec((tm, GM_WIDTH), row(1)),
            pl.BlockSpec((tm, GM_WIDTH), row(2)),
            pl.BlockSpec((1, SSD_WIDTH), const2),
            pl.BlockSpec((1, SSD_WIDTH), const2),
            pl.BlockSpec((1, GM_WIDTH), const2),
            pl.BlockSpec((GM_GROUPS, GM_CHUNK, GM_CHUNK), lambda i: (0, 0, 0)),
            pl.BlockSpec((GM_CHUNK, GM_GROUPS), const2),
        ],
        out_specs=pl.BlockSpec((tm, SSD_WIDTH + GM_WIDTH), lambda i: (i, 0)),
        out_shape=jax.ShapeDtypeStruct((rows, SSD_WIDTH + GM_WIDTH), BF16),
        compiler_params=_cparams(("arbitrary",)),
        name="mixer_merge",
    )(yf, yb, xbc, proj, proj, proj, d_exp, ssd_g, v_g, w_s, b_s_t)


def _attn_kernel(qn_ref, qr_ref, kn_ref, kr_ref, v_ref, o_ref, *, tk, n_kv):
    qn = qn_ref[0]
    qr = qr_ref[0]
    tq = qn.shape[0]
    nt_dims = (((1,), (1,)), ((), ()))

    def body(c, carry):
        m, l, acc = carry
        off = pl.multiple_of(c * tk, tk)
        s = lax.dot_general(qn, kn_ref[0, pl.ds(off, tk), :], nt_dims, preferred_element_type=F32)
        s = s + lax.dot_general(qr, kr_ref[0, pl.ds(off, tk), :], nt_dims, preferred_element_type=F32)
        s = s * MLA_SCALE
        m_new = jnp.maximum(m, jnp.max(s, axis=-1, keepdims=True))
        alpha = jnp.exp(m - m_new)
        p = jnp.exp(s - m_new)
        l = alpha * l + jnp.sum(p, axis=-1, keepdims=True)
        acc = alpha * acc + jnp.dot(p.astype(BF16), v_ref[0, pl.ds(off, tk), :], preferred_element_type=F32)
        return m_new, l, acc

    init = (jnp.full((tq, 1), -jnp.inf, F32), jnp.zeros((tq, 1), F32), jnp.zeros((tq, V_DIM), F32))
    _, l, acc = lax.fori_loop(0, n_kv, body, init)
    o_ref[0] = (acc / l).astype(o_ref.dtype)


def _attention(qn, qr, kv, kr, *, tq, tk):
    bsz, nq, _ = qn.shape
    nt = kv.shape[1]
    return pl.pallas_call(
        functools.partial(_attn_kernel, tk=tk, n_kv=nt // tk),
        grid=(bsz, MLA_HEADS, nq // tq),
        in_specs=[
            pl.BlockSpec((1, tq, QK_NOPE), lambda b, h, i: (b, i, h)),
            pl.BlockSpec((1, tq, LANE), lambda b, h, i: (b, i, h)),
            pl.BlockSpec((1, nt, QK_NOPE), lambda b, h, i: (b, 0, 2 * h)),
            pl.BlockSpec((1, nt, LANE), lambda b, h, i: (b, 0, 0)),
            pl.BlockSpec((1, nt, V_DIM), lambda b, h, i: (b, 0, 2 * h + 1)),
        ],
        out_specs=pl.BlockSpec((1, tq, V_DIM), lambda b, h, i: (b, i, h)),
        out_shape=jax.ShapeDtypeStruct((bsz, nq, MLA_HEADS * V_DIM), BF16),
        compiler_params=_cparams(("arbitrary", "arbitrary", "arbitrary")),
        name="mla_attention",
    )(qn, qr, kv, kr, kv)


def _ffn_pre_kernel(x_ref, g_ref, mod_ref, rw_ref, rb_ref, h_ref, lg_ref, *, tm, rows_per_batch, n_ctx):
    i = pl.program_id(0)
    is_ctx = _is_ctx_rows(i, tm, rows_per_batch, n_ctx)
    h = _norm_rows(x_ref[...], g_ref[...])
    h = h * (1.0 + _mod_select(mod_ref, 4, is_ctx)) + _mod_select(mod_ref, 3, is_ctx)
    h_ref[...] = h
    lg_ref[...] = jnp.dot(h, rw_ref[...], precision=HIGHEST, preferred_element_type=F32) + rb_ref[...]


def _ffn_pre(x, g, mod, rw_pad, rb_pad, *, tm, rows_per_batch, n_ctx):
    rows, d = x.shape
    batch_of = lambda i: (i * tm) // rows_per_batch
    return pl.pallas_call(
        functools.partial(_ffn_pre_kernel, tm=tm, rows_per_batch=rows_per_batch, n_ctx=n_ctx),
        grid=(rows // tm,),
        in_specs=[
            pl.BlockSpec((tm, d), lambda i: (i, 0)),
            pl.BlockSpec((1, d), lambda i: (0, 0)),
            pl.BlockSpec((1, 2, 6, d), lambda i: (batch_of(i), 0, 0, 0)),
            pl.BlockSpec((d, LANE), lambda i: (0, 0)),
            pl.BlockSpec((1, LANE), lambda i: (0, 0)),
        ],
        out_specs=[pl.BlockSpec((tm, d), lambda i: (i, 0)), pl.BlockSpec((tm, LANE), lambda i: (i, 0))],
        out_shape=[jax.ShapeDtypeStruct((rows, d), F32), jax.ShapeDtypeStruct((rows, LANE), F32)],
        compiler_params=_cparams(("arbitrary",)),
        name="ffn_prenorm_router",
    )(x, g.reshape(1, d), mod, rw_pad, rb_pad)


def _row_copy(src_hbm, row, buf, slot, sem):
    return pltpu.make_async_copy(src_hbm.at[pl.ds(row, 1)], buf.at[pl.ds(slot, 1)], sem)


def _gather_rows_kernel(nv_ref, idx_ref, src_hbm, o_ref, buf, sem, *, bm):
    @pl.when(pl.program_id(0) < nv_ref[0])
    def _():
        def start(r, c):
            _row_copy(src_hbm, idx_ref[0, 0, r], buf, r, sem).start()
            return c
        lax.fori_loop(0, bm, start, 0, unroll=8)

        def wait(r, c):
            _row_copy(src_hbm, 0, buf, r, sem).wait()
            return c
        lax.fori_loop(0, bm, wait, 0, unroll=8)
        o_ref[...] = buf[...].astype(o_ref.dtype)

    @pl.when(pl.program_id(0) >= nv_ref[0])
    def _():
        o_ref[...] = jnp.zeros_like(o_ref)


def _gather_rows(src, idx, n_valid, *, bm, out_dtype):
    n_slots = idx.shape[0]
    d = src.shape[1]
    nb = n_slots // bm
    clamp = lambda i, nv: jnp.minimum(i, nv[0] - 1)
    return pl.pallas_call(
        functools.partial(_gather_rows_kernel, bm=bm),
        grid_spec=pltpu.PrefetchScalarGridSpec(
            num_scalar_prefetch=1,
            grid=(nb,),
            in_specs=[
                pl.BlockSpec((1, 1, bm), lambda i, nv: (clamp(i, nv), 0, 0), memory_space=pltpu.SMEM),
                pl.BlockSpec(memory_space=pl.ANY),
            ],
            out_specs=pl.BlockSpec((bm, d), lambda i, nv: (i, 0)),
            scratch_shapes=[pltpu.VMEM((bm, d), src.dtype), pltpu.SemaphoreType.DMA(())],
        ),
        out_shape=jax.ShapeDtypeStruct((n_slots, d), out_dtype),
        compiler_params=_cparams(("arbitrary",)),
        name="moe_gather_rows",
    )(n_valid, idx.reshape(nb, 1, bm), src)


def _moe_up_kernel(be_ref, nv_ref, x_ref, wg_ref, wl_ref, bg_ref, bl_ref, o_ref):
    @pl.when(pl.program_id(1) < nv_ref[0])
    def _():
        x = x_ref[...]
        a_g = jnp.dot(x, wg_ref[...], preferred_element_type=F32) + bg_ref[...]
        a_l = jnp.dot(x, wl_ref[...], preferred_element_type=F32) + bl_ref[...]
        glu = jnp.minimum(a_g, SWIGLU_LIMIT)
        lin = jnp.clip(a_l, -SWIGLU_LIMIT, SWIGLU_LIMIT)
        o_ref[...] = (glu * jax.nn.sigmoid(SWIGLU_ALPHA * glu) * (lin + 1.0)).astype(o_ref.dtype)

    @pl.when(pl.program_id(1) >= nv_ref[0])
    def _():
        o_ref[...] = jnp.zeros_like(o_ref)


def _moe_up(xs, w_glu, w_lin, b_glu, b_lin, block_expert, n_valid, *, tn):
    n_slots, d = xs.shape
    ff = w_glu.shape[2]
    nb = n_slots // MOE_ROWS
    blk = lambda i, nv: jnp.minimum(i, nv[0] - 1)
    wmap = lambda j, i, be, nv: (be[blk(i, nv)], 0, j)
    return pl.pallas_call(
        _moe_up_kernel,
        grid_spec=pltpu.PrefetchScalarGridSpec(
            num_scalar_prefetch=2,
            grid=(ff // tn, nb),
            in_specs=[
                pl.BlockSpec((MOE_ROWS, d), lambda j, i, be, nv: (blk(i, nv), 0)),
                pl.BlockSpec((None, d, tn), wmap),
                pl.BlockSpec((None, d, tn), wmap),
                pl.BlockSpec((None, 1, tn), wmap),
                pl.BlockSpec((None, 1, tn), wmap),
            ],
            out_specs=pl.BlockSpec((MOE_ROWS, tn), lambda j, i, be, nv: (i, j)),
        ),
        out_shape=jax.ShapeDtypeStruct((n_slots, ff), BF16),
        compiler_params=_cparams(("arbitrary", "arbitrary")),
        name="moe_up",
    )(block_expert, n_valid, xs, w_glu, w_lin, b_glu, b_lin)


def _moe_down_kernel(be_ref, nv_ref, a_ref, w_ref, b_ref, o_ref):
    @pl.when(pl.program_id(1) < nv_ref[0])
    def _():
        o_ref[...] = jnp.dot(a_ref[...], w_ref[...], preferred_element_type=F32) + b_ref[...]

    @pl.when(pl.program_id(1) >= nv_ref[0])
    def _():
        o_ref[...] = jnp.zeros_like(o_ref)


def _moe_down(act, w2, b2, block_expert, n_valid, *, tn):
    n_slots, ff = act.shape
    d = w2.shape[2]
    nb = n_slots // MOE_ROWS
    blk = lambda i, nv: jnp.minimum(i, nv[0] - 1)
    wmap = lambda j, i, be, nv: (be[blk(i, nv)], 0, j)
    return pl.pallas_call(
        _moe_down_kernel,
        grid_spec=pltpu.PrefetchScalarGridSpec(
            num_scalar_prefetch=2,
            grid=(d // tn, nb),
            in_specs=[
                pl.BlockSpec((MOE_ROWS, ff), lambda j, i, be, nv: (blk(i, nv), 0)),
                pl.BlockSpec((None, ff, tn), wmap),
                pl.BlockSpec((None, 1, tn), wmap),
            ],
            out_specs=pl.BlockSpec((MOE_ROWS, tn), lambda j, i, be, nv: (i, j)),
        ),
        out_shape=jax.ShapeDtypeStruct((n_slots, d), F32),
        compiler_params=_cparams(("arbitrary", "arbitrary")),
        name="moe_down",
    )(block_expert, n_valid, act, w2, b2)


def _moe_combine_kernel(pos_ref, y_hbm, gate_ref, x_ref, mod_ref, fg_ref, o_ref, buf, sem, *,
                        tm, rows_per_batch, n_ctx, final_norm):
    n_rows = TOP_K * tm

    def start(r, c):
        slot = (r % TOP_K) * tm + r // TOP_K
        _row_copy(y_hbm, pos_ref[0, 0, r], buf, slot, sem).start()
        return c
    lax.fori_loop(0, n_rows, start, 0, unroll=8)

    def wait(r, c):
        _row_copy(y_hbm, 0, buf, r, sem).wait()
        return c
    lax.fori_loop(0, n_rows, wait, 0, unroll=8)

    gates = gate_ref[...]
    f = gates[:, 0:1] * buf[0:tm, :]
    for kk in range(1, TOP_K):
        f = f + gates[:, kk:kk + 1] * buf[kk * tm:(kk + 1) * tm, :]
    is_ctx = _is_ctx_rows(pl.program_id(0), tm, rows_per_batch, n_ctx)
    out = x_ref[...] + _mod_select(mod_ref, 5, is_ctx) * f
    if final_norm:
        out = _norm_rows(out, fg_ref[...])
    o_ref[...] = out


def _moe_combine(y2, pos, gates, x, mod, final_g, *, tm, rows_per_batch, n_ctx, final_norm):
    rows, d = x.shape
    nblk = rows // tm
    batch_of = lambda i: (i * tm) // rows_per_batch
    return pl.pallas_call(
        functools.partial(_moe_combine_kernel, tm=tm, rows_per_batch=rows_per_batch, n_ctx=n_ctx,
                          final_norm=final_norm),
        grid=(nblk,),
        in_specs=[
            pl.BlockSpec((1, 1, TOP_K * tm), lambda i: (i, 0, 0), memory_space=pltpu.SMEM),
            pl.BlockSpec(memory_space=pl.ANY),
            pl.BlockSpec((tm, TOP_K), lambda i: (i, 0)),
            pl.BlockSpec((tm, d), lambda i: (i, 0)),
            pl.BlockSpec((1, 2, 6, d), lambda i: (batch_of(i), 0, 0, 0)),
            pl.BlockSpec((1, d), lambda i: (0, 0)),
        ],
        out_specs=pl.BlockSpec((tm, d), lambda i: (i, 0)),
        out_shape=jax.ShapeDtypeStruct((rows, d), F32),
        scratch_shapes=[pltpu.VMEM((TOP_K * tm, d), F32), pltpu.SemaphoreType.DMA(())],
        compiler_params=_cparams(("arbitrary",)),
        name="moe_combine",
    )(pos.reshape(nblk, 1, TOP_K * tm), y2, gates, x, mod, final_g.reshape(1, d))


def _route(logits, n_experts):
    t_tok = logits.shape[0]
    top_val, top_idx = lax.top_k(logits, TOP_K)
    gates = jax.nn.softmax(top_val, axis=-1)
    flat_e = top_idx.reshape(-1)
    order = jnp.argsort(flat_e)
    e_sorted = flat_e[order]
    counts = jnp.bincount(flat_e, length=n_experts)
    padded = (counts + MOE_ROWS - 1) // MOE_ROWS * MOE_ROWS
    cum_pad = jnp.cumsum(padded)
    pad_start = cum_pad - padded
    start = jnp.cumsum(counts) - counts
    n_assign = t_tok * TOP_K
    dest = (pad_start[e_sorted] + jnp.arange(n_assign) - start[e_sorted]).astype(jnp.int32)
    n_blocks = n_assign // MOE_ROWS + n_experts
    slot_tok = jnp.zeros((n_blocks * MOE_ROWS,), jnp.int32).at[dest].set((order // TOP_K).astype(jnp.int32))
    pos = jnp.zeros((n_assign,), jnp.int32).at[order].set(dest)
    block_expert = jnp.clip(jnp.searchsorted(cum_pad, jnp.arange(n_blocks) * MOE_ROWS, side="right"),
                            0, n_experts - 1).astype(jnp.int32)
    n_valid = (cum_pad[-1:] // MOE_ROWS).astype(jnp.int32)
    return gates, slot_tok, pos, block_expert, n_valid


def _moe_layer(x, norm_g, mod, router_w, router_b, w1, b1, w2, b2, final_g, *,
               rows_per_batch, n_ctx, tm_rows, final_norm):
    rows, d = x.shape
    n_experts = router_w.shape[1]
    ff = w2.shape[1]
    rw_pad = jnp.pad(router_w, ((0, 0), (0, LANE - n_experts)))
    rb_pad = jnp.pad(router_b, (0, LANE - n_experts)).reshape(1, LANE)
    hf, logits = _ffn_pre(x, norm_g, mod, rw_pad, rb_pad, tm=tm_rows, rows_per_batch=rows_per_batch, n_ctx=n_ctx)
    gates, slot_tok, pos, block_expert, n_valid = _route(logits[:, :n_experts], n_experts)
    xs = _gather_rows(hf, slot_tok, n_valid, bm=MOE_ROWS, out_dtype=BF16)
    w_glu = w1[:, :, 0::2].astype(BF16)
    w_lin = w1[:, :, 1::2].astype(BF16)
    b_glu = b1[:, 0::2].reshape(n_experts, 1, ff)
    b_lin = b1[:, 1::2].reshape(n_experts, 1, ff)
    act = _moe_up(xs, w_glu, w_lin, b_glu, b_lin, block_expert, n_valid, tn=_pick(ff, (1024, 512, 256, 128)))
    y2 = _moe_down(act, w2.astype(BF16), b2.reshape(n_experts, 1, d), block_expert, n_valid,
                   tn=_pick(d, (1024, 512, 256, 128)))
    tm_c = _pick(rows_per_batch, (128,))
    return _moe_combine(y2, pos, gates, x, mod, final_g, tm=tm_c, rows_per_batch=rows_per_batch,
                        n_ctx=n_ctx, final_norm=final_norm)


def _modulation(c, c_ctx, ada_w_i, ada_b_i):
    bsz, d = c.shape
    rows = 16
    s = jnp.concatenate([jax.nn.silu(c), jax.nn.silu(c_ctx)[None], jnp.zeros((rows - bsz - 1, d), F32)], axis=0)
    n = ada_w_i.shape[1]
    m = _fused_matmul(s, ada_w_i, name="adaln_modulation", m_rows=rows, k=d, tm=rows, tn=_pick(n, (1024, 512, 256, 128)), out_dtype=F32,
                      rows_per_batch=rows, epilogue="bias", bias=ada_b_i)
    lat = m[:bsz].reshape(bsz, 6, d)
    ctx = jnp.broadcast_to(m[bsz].reshape(1, 6, d), (bsz, 6, d))
    return jnp.stack([ctx, lat], axis=1)


def _dwconv_silu(xbc, conv_w, conv_b, n_ctx):
    nt = xbc.shape[1]
    t = jnp.arange(nt)[None, :, None]
    prev = jnp.pad(xbc, ((0, 0), (1, 0), (0, 0)))[:, :nt]
    nxt = jnp.pad(xbc, ((0, 0), (0, 1), (0, 0)))[:, 1:]
    prev = jnp.where((t == 0) | (t == n_ctx), 0.0, prev)
    nxt = jnp.where((t == n_ctx - 1) | (t == nt - 1), 0.0, nxt)
    out = conv_b + prev * conv_w[0] + xbc * conv_w[1] + nxt * conv_w[2]
    return jax.nn.silu(out)


def _rope_tables(n_lat):
    rows = n_lat // GRID_W
    row = jnp.broadcast_to(jnp.arange(rows)[:, None], (rows, GRID_W)).reshape(-1)
    col = jnp.broadcast_to(jnp.arange(GRID_W)[None, :], (rows, GRID_W)).reshape(-1)
    freqs = ROPE_THETA ** (-jnp.arange(ROPE_PAIRS, dtype=F32) / ROPE_PAIRS)
    ang = jnp.stack([row[:, None] * freqs, col[:, None] * freqs], axis=1)
    cos, sin = jnp.cos(ang).astype(F32), jnp.sin(ang).astype(F32)
    c_tab = jnp.concatenate([cos[:, 0], cos[:, 0], cos[:, 1], cos[:, 1]], axis=-1)
    s_tab = jnp.concatenate([-sin[:, 0], sin[:, 0], -sin[:, 1], sin[:, 1]], axis=-1)
    return c_tab, s_tab


def _swap16(w):
    lane = jnp.arange(w.shape[-1])
    return jnp.take(w, jnp.where(lane % 32 < 16, lane + 16, lane - 16), axis=-1)


def kernel(x, c, ctx, c_ctx, mix_norm_g, ffn_norm_g, ada_w, ada_b, hyb_w_in, hyb_conv_w, hyb_conv_b, hyb_dt_bias, hyb_a_log, hyb_d_skip, hyb_ssd_norm_g, hyb_v_norm_g, hyb_w_s, hyb_b_s, hyb_w_out, mla_w_in, mla_q_norm_g, mla_kv_norm_g, mla_w_uq, mla_w_ukv, mla_w_o, router_w, router_b, exp_w1, exp_b1, exp_w2, exp_b2, final_norm_g):
    bsz, n_lat, d = x.shape
    n_ctx = ctx.shape[1]
    nt = n_ctx + n_lat
    rows = bsz * nt
    tm_s = _pick(nt, (768, 384, 256, 128))
    tm_l = _pick(n_lat, (1024, 512, 256, 128))

    stream = jnp.concatenate([ctx, x], axis=1).reshape(rows, d)

    mod0 = _modulation(c, c_ctx, ada_w[0], ada_b[0])
    cuts = (SSD_WIDTH, SSD_WIDTH + XBC_WIDTH, SSD_WIDTH + XBC_WIDTH + 2 * SSD_HEADS)
    w_in = hyb_w_in[0]
    w_main = jnp.concatenate([w_in[:, :cuts[0]], w_in[:, cuts[2]:], w_in[:, cuts[0]:cuts[1]]], axis=1).astype(BF16)
    w_dt = jnp.pad(w_in[:, cuts[1]:cuts[2]], ((0, 0), (0, LANE - 2 * SSD_HEADS))).astype(BF16)
    pro = dict(prologue="norm_mod", g=mix_norm_g[0], mod=mod0, sh_idx=0, sc_idx=1,
               rows_per_batch=nt, n_ctx=n_ctx, m_rows=rows, k=d, tm=tm_s)
    proj = _fused_matmul(stream, w_main, name="hyb_in_proj", tn=_pick(w_main.shape[1], (1024, 512, 256, 128)), out_dtype=F32, **pro)
    dt_raw = _fused_matmul(stream, w_dt, name="hyb_dt_proj", tn=LANE, out_dtype=F32, **pro)

    xbc_off = SSD_WIDTH + 2 * GM_WIDTH
    xbc = _dwconv_silu(proj[:, xbc_off:].reshape(bsz, nt, XBC_WIDTH), hyb_conv_w[0], hyb_conv_b[0], n_ctx)
    dt_bias = jnp.pad(hyb_dt_bias[0].reshape(1, -1), ((0, 0), (0, LANE - 2 * SSD_HEADS)))
    a_log = jnp.pad(hyb_a_log[0].reshape(1, -1), ((0, 0), (0, LANE - 2 * SSD_HEADS)))
    dt3 = dt_raw.reshape(bsz, nt, LANE)
    n_ctx_chunks = n_ctx // SSD_CHUNK
    y_f = _ssd_scan(xbc, dt3, dt_bias, a_log, rev=False, n_ctx_chunks=n_ctx_chunks)
    y_b = _ssd_scan(xbc, dt3, dt_bias, a_log, rev=True, n_ctx_chunks=n_ctx_chunks)

    d_exp = jnp.repeat(hyb_d_skip[0], SSD_HEAD_DIM).reshape(1, SSD_WIDTH)
    y_cat = _mixer_merge(y_f.reshape(rows, SSD_WIDTH), y_b.reshape(rows, SSD_WIDTH),
                         xbc.reshape(rows, XBC_WIDTH), proj, d_exp,
                         hyb_ssd_norm_g[0].reshape(1, -1), hyb_v_norm_g[0].reshape(1, -1),
                         hyb_w_s[0].astype(BF16), hyb_b_s[0].T)
    stream = _fused_matmul(y_cat, hyb_w_out[0].astype(BF16), name="hyb_out_proj", m_rows=rows, k=SSD_WIDTH + GM_WIDTH, tm=tm_s,
                           tn=_pick(d, (1024, 512, 256, 128)), out_dtype=F32, rows_per_batch=nt, n_ctx=n_ctx,
                           epilogue="gated_resid", resid=stream, mod=mod0, gate_idx=2)
    stream = _moe_layer(stream, ffn_norm_g[0], mod0, router_w[0], router_b[0], exp_w1[0], exp_b1[0],
                        exp_w2[0], exp_b2[0], final_norm_g, rows_per_batch=nt, n_ctx=n_ctx, tm_rows=tm_s,
                        final_norm=False)

    mod1 = _modulation(c, c_ctx, ada_w[1], ada_b[1])
    w_in1 = mla_w_in[0]
    kpe_w = w_in1[:, Q_LORA + KV_LORA:]
    w_in1_r = jnp.concatenate([w_in1[:, Q_LORA:Q_LORA + KV_LORA], kpe_w, _swap16(kpe_w),
                               jnp.zeros((d, Q_LORA - KV_LORA - 2 * QK_ROPE), F32), w_in1[:, :Q_LORA]],
                              axis=1).astype(BF16)
    p1 = _fused_matmul(stream, w_in1_r, name="mla_in_proj", m_rows=rows, k=d, tm=tm_s, tn=w_in1_r.shape[1], out_dtype=F32,
                       prologue="norm_mod", g=mix_norm_g[1], mod=mod1, sh_idx=0, sc_idx=1,
                       rows_per_batch=nt, n_ctx=n_ctx)

    c_tab, s_tab = _rope_tables(n_lat)
    ck = jnp.concatenate([jnp.ones((n_ctx, QK_ROPE), F32), c_tab], axis=0)
    sk = jnp.concatenate([jnp.zeros((n_ctx, QK_ROPE), F32), s_tab], axis=0)
    p1_3 = p1.reshape(bsz, nt, -1)
    k_rot = p1_3[:, :, KV_LORA:KV_LORA + QK_ROPE] * ck + p1_3[:, :, KV_LORA + QK_ROPE:KV_LORA + 2 * QK_ROPE] * sk
    kr = jnp.concatenate([k_rot, k_rot], axis=-1).astype(BF16)

    kv = _fused_matmul(p1, mla_w_ukv[0].astype(BF16), name="mla_kv_up", m_rows=rows, k=KV_LORA, tm=tm_s,
                       tn=_pick(mla_w_ukv.shape[2], (2048, 1024, 512, 256, 128)), out_dtype=BF16,
                       rows_per_batch=nt, prologue="norm", g=mla_kv_norm_g[0], x_col_block=0)

    w_uq = mla_w_uq[0].reshape(Q_LORA, MLA_HEADS, QK_NOPE + QK_ROPE)
    w_q_nope = w_uq[:, :, :QK_NOPE].reshape(Q_LORA, MLA_HEADS * QK_NOPE).astype(BF16)
    w_q_rope = jnp.concatenate([w_uq[:, :, QK_NOPE:], _swap16(w_uq[:, :, QK_NOPE:])], axis=-1)
    w_q_rope = w_q_rope.reshape(Q_LORA, MLA_HEADS * 2 * QK_ROPE).astype(BF16)
    tm_q = _pick(n_ctx, (256, 128))
    lat_blocks = n_lat // tm_q
    q_rows = lambda i: (i // lat_blocks) * (nt // tm_q) + n_ctx // tm_q + i % lat_blocks
    qpro = dict(m_rows=bsz * n_lat, k=Q_LORA, tm=tm_q, out_dtype=BF16, rows_per_batch=n_lat,
                prologue="norm", g=mla_q_norm_g[0], x_row_map=q_rows, x_col_block=1)
    qn = _fused_matmul(p1, w_q_nope, name="mla_q_nope", tn=w_q_nope.shape[1], **qpro)
    cs_q = jnp.concatenate([c_tab, s_tab], axis=-1)
    qr = _fused_matmul(p1, w_q_rope, name="mla_q_rope", tn=w_q_rope.shape[1], epilogue="mul_table", table=cs_q,
                       table_row_map=lambda i: i % lat_blocks, **qpro)

    o = _attention(qn.reshape(bsz, n_lat, -1), qr.reshape(bsz, n_lat, -1), kv.reshape(bsz, nt, -1), kr,
                   tq=_pick(n_lat, (512, 256, 128)), tk=_pick(nt, (768, 512, 384, 256, 128)))

    lat = stream.reshape(bsz, nt, d)[:, n_ctx:].reshape(bsz * n_lat, d)
    mod1_lat = mod1
    lat = _fused_matmul(o.reshape(bsz * n_lat, -1), mla_w_o[0].astype(BF16), name="mla_out_proj",
                        m_rows=bsz * n_lat,
                        k=MLA_HEADS * V_DIM, tm=tm_l, tn=_pick(d, (1024, 512, 256, 128)), out_dtype=F32,
                        rows_per_batch=n_lat, epilogue="gated_resid", resid=lat, mod=mod1_lat, gate_idx=2)
    out = _moe_layer(lat, ffn_norm_g[1], mod1_lat, router_w[1], router_b[1], exp_w1[1], exp_b1[1],
                     exp_w2[1], exp_b2[1], final_norm_g, rows_per_batch=n_lat, n_ctx=0, tm_rows=tm_l,
                     final_norm=True)
    return out.reshape(bsz, n_lat, d)
```

```python
import functools
import math

import jax
import jax.numpy as jnp
from jax import lax
from jax.experimental import pallas as pl
from jax.experimental.pallas import tpu as pltpu

F32 = jnp.float32
BF16 = jnp.bfloat16
HIGHEST = lax.Precision.HIGHEST
EPS = 1e-6

GRID_W = 64
SSD_HEADS = 32
SSD_HEAD_DIM = 64
SSD_WIDTH = SSD_HEADS * SSD_HEAD_DIM
SSD_GROUPS = 4
SSD_HEADS_PER_GROUP = SSD_HEADS // SSD_GROUPS
SSD_STATE = 128
SSD_CHUNK = 128
SSD_GROUP_WIDTH = SSD_HEADS_PER_GROUP * SSD_HEAD_DIM
BC_WIDTH = SSD_GROUPS * SSD_STATE
XBC_WIDTH = SSD_WIDTH + 2 * BC_WIDTH
GM_GROUPS = 16
GM_GROUP_DIM = 128
GM_WIDTH = GM_GROUPS * GM_GROUP_DIM
GM_CHUNK = 128
MLA_HEADS = 16
Q_LORA = 768
KV_LORA = 512
QK_NOPE = 128
QK_ROPE = 64
V_DIM = 128
ROPE_PAIRS = QK_ROPE // 4
ROPE_THETA = 10000.0
MLA_SCALE = (QK_NOPE + QK_ROPE) ** -0.5
TOP_K = 4
SWIGLU_LIMIT = 7.0
SWIGLU_ALPHA = 1.702
MOE_ROWS = 256

LANE = 128
VMEM_LIMIT = 56 * 1024 * 1024


def _pick(n, candidates):
    for c in candidates:
        if n % c == 0:
            return c
    raise ValueError(f"no tile for {n} in {candidates}")


def _cparams(sem):
    return pltpu.CompilerParams(dimension_semantics=sem, vmem_limit_bytes=VMEM_LIMIT)


def _is_ctx_rows(i, tm, rows_per_batch, n_ctx):
    row = (i * tm) % rows_per_batch + lax.broadcasted_iota(jnp.int32, (tm, 1), 0)
    return row < n_ctx


def _mod_select(mod_ref, idx, is_ctx):
    return jnp.where(is_ctx, mod_ref[0, 0, idx:idx + 1, :], mod_ref[0, 1, idx:idx + 1, :])


def _norm_rows(x, g):
    y = x * lax.rsqrt(jnp.mean(x * x, axis=-1, keepdims=True) + EPS)
    return y * g


def _fused_matmul_kernel(*refs, prologue, epilogue, tm, tn, rows_per_batch, n_ctx,
                         sh_idx, sc_idx, gate_idx, use_scratch):
    it = iter(refs)
    x_ref = next(it)
    g_ref = next(it) if prologue != "none" else None
    pmod_ref = next(it) if prologue == "norm_mod" else None
    w_ref = next(it)
    bias_ref = next(it) if epilogue == "bias" else None
    resid_ref = next(it) if epilogue == "gated_resid" else None
    emod_ref = next(it) if epilogue == "gated_resid" else None
    tab_ref = next(it) if epilogue == "mul_table" else None
    o_ref = next(it)
    xn_ref = next(it) if use_scratch else None

    i = pl.program_id(0)
    j = pl.program_id(1)

    if use_scratch:
        @pl.when(j == 0)
        def _():
            x = x_ref[...].astype(F32)
            if prologue != "none":
                x = _norm_rows(x, g_ref[...])
            if prologue == "norm_mod":
                is_ctx = _is_ctx_rows(i, tm, rows_per_batch, n_ctx)
                x = x * (1.0 + _mod_select(pmod_ref, sc_idx, is_ctx)) + _mod_select(pmod_ref, sh_idx, is_ctx)
            xn_ref[...] = x.astype(BF16)
        xn = xn_ref[...]
    else:
        xn = x_ref[...]

    acc = jnp.dot(xn, w_ref[...].astype(BF16), preferred_element_type=F32)
    if epilogue == "bias":
        acc = acc + bias_ref[...]
    elif epilogue == "gated_resid":
        is_ctx = _is_ctx_rows(i, tm, rows_per_batch, n_ctx)
        acc = resid_ref[...] + _mod_select(emod_ref, gate_idx, is_ctx) * acc
    elif epilogue == "mul_table":
        acc = acc * jnp.tile(tab_ref[...], (1, tn // tab_ref.shape[1]))
    o_ref[...] = acc.astype(o_ref.dtype)


def _fused_matmul(x, w, *, name, m_rows, k, tm, tn, out_dtype, rows_per_batch, n_ctx=0,
                  prologue="none", g=None, mod=None, sh_idx=0, sc_idx=0,
                  epilogue="none", bias=None, resid=None, gate_idx=0, table=None,
                  x_row_map=None, x_col_block=0, resid_row_map=None, table_row_map=None):
    n = w.shape[1]
    assert w.shape[0] == k and m_rows % tm == 0 and n % tn == 0 and rows_per_batch % tm == 0
    ident = lambda i: i
    x_row_map = x_row_map or ident
    resid_row_map = resid_row_map or ident
    table_row_map = table_row_map or ident
    batch_of = lambda i: (i * tm) // rows_per_batch
    use_scratch = prologue != "none" or x.dtype != BF16

    args = [x]
    in_specs = [pl.BlockSpec((tm, k), lambda i, j: (x_row_map(i), x_col_block))]
    if prologue != "none":
        args.append(g.reshape(1, k))
        in_specs.append(pl.BlockSpec((1, k), lambda i, j: (0, 0)))
    if prologue == "norm_mod":
        args.append(mod)
        in_specs.append(pl.BlockSpec((1, 2, 6, k), lambda i, j: (batch_of(i), 0, 0, 0)))
    args.append(w)
    in_specs.append(pl.BlockSpec((k, tn), lambda i, j: (0, j)))
    if epilogue == "bias":
        args.append(bias.reshape(1, n))
        in_specs.append(pl.BlockSpec((1, tn), lambda i, j: (0, j)))
    if epilogue == "gated_resid":
        args.append(resid)
        in_specs.append(pl.BlockSpec((tm, tn), lambda i, j: (resid_row_map(i), j)))
        args.append(mod)
        in_specs.append(pl.BlockSpec((1, 2, 6, tn), lambda i, j: (batch_of(i), 0, 0, j)))
    if epilogue == "mul_table":
        args.append(table)
        in_specs.append(pl.BlockSpec((tm, table.shape[1]), lambda i, j: (table_row_map(i), 0)))

    kern = functools.partial(
        _fused_matmul_kernel, prologue=prologue, epilogue=epilogue, tm=tm, tn=tn,
        rows_per_batch=rows_per_batch, n_ctx=n_ctx, sh_idx=sh_idx, sc_idx=sc_idx,
        gate_idx=gate_idx, use_scratch=use_scratch)
    return pl.pallas_call(
        kern,
        grid=(m_rows // tm, n // tn),
        in_specs=in_specs,
        out_specs=pl.BlockSpec((tm, tn), lambda i, j: (i, j)),
        out_shape=jax.ShapeDtypeStruct((m_rows, n), out_dtype),
        scratch_shapes=[pltpu.VMEM((tm, k), BF16)] if use_scratch else [],
        compiler_params=_cparams(("arbitrary", "arbitrary")),
        name=name,
    )(*args)


def _expand_heads(mat, col0):
    rows = mat.shape[0]
    return jnp.concatenate(
        [jnp.broadcast_to(mat[:, col0 + jj:col0 + jj + 1], (rows, SSD_HEAD_DIM))
         for jj in range(SSD_HEADS_PER_GROUP)], axis=1)


def _ssd_kernel(xs_ref, b_ref, c_ref, dtr_ref, dtb_ref, alog_ref, y_ref, h_ref, *, rev):
    L = SSD_CHUNK
    step = pl.program_id(1)

    @pl.when(step == 0)
    def _():
        h_ref[...] = jnp.zeros_like(h_ref)

    dcol = SSD_HEADS if rev else 0
    dt = jax.nn.softplus(dtr_ref[0] + dtb_ref[...])
    a = dt * (-jnp.exp(alog_ref[...]))
    r_i = lax.broadcasted_iota(jnp.int32, (L, L), 0)
    c_i = lax.broadcasted_iota(jnp.int32, (L, L), 1)
    tri = (r_i <= c_i) if rev else (r_i >= c_i)
    acs = jnp.dot(tri.astype(F32), a, precision=HIGHEST, preferred_element_type=F32)
    acs_t = acs.T
    edge = acs[0:1, :] if rev else acs[L - 1:L, :]
    decay_end = jnp.exp(edge - acs)
    exp_acs = jnp.exp(acs)
    chunk_decay = jnp.exp(edge)

    for g in range(SSD_GROUPS):
        col0 = dcol + g * SSD_HEADS_PER_GROUP
        bg = b_ref[0, :, g * SSD_STATE:(g + 1) * SSD_STATE]
        cg = c_ref[0, :, g * SSD_STATE:(g + 1) * SSD_STATE].astype(BF16)
        cb = lax.dot_general(cg, bg.astype(BF16), (((1,), (1,)), ((), ())), preferred_element_type=F32)
        xdt = xs_ref[0, :, g * SSD_GROUP_WIDTH:(g + 1) * SSD_GROUP_WIDTH] * _expand_heads(dt, col0)
        y_diag = []
        for jj in range(SSD_HEADS_PER_GROUP):
            col = col0 + jj
            seg = acs[:, col:col + 1] - acs_t[col:col + 1, :]
            lmat = jnp.exp(jnp.where(tri, seg, -jnp.inf))
            y_diag.append(jnp.dot((cb * lmat).astype(BF16),
                                  xdt[:, jj * SSD_HEAD_DIM:(jj + 1) * SSD_HEAD_DIM].astype(BF16),
                                  preferred_element_type=F32))
        y_diag = jnp.concatenate(y_diag, axis=1)
        h_g = h_ref[g]
        y_off = jnp.dot(cg, h_g.astype(BF16), preferred_element_type=F32) * _expand_heads(exp_acs, col0)
        y_ref[0, :, g * SSD_GROUP_WIDTH:(g + 1) * SSD_GROUP_WIDTH] = y_diag + y_off
        states = jnp.dot(bg.T.astype(BF16), (xdt * _expand_heads(decay_end, col0)).astype(BF16),
                         preferred_element_type=F32)
        h_ref[g] = h_g * _expand_heads(chunk_decay, col0) + states


def _ssd_scan(xbc, dt_raw, dt_bias, a_log, *, rev, n_ctx_chunks):
    bsz, nt, _ = xbc.shape
    nc = nt // SSD_CHUNK
    if rev:
        cmap = lambda s: jnp.where(s < n_ctx_chunks, n_ctx_chunks - 1 - s, nc - 1 + n_ctx_chunks - s)
    else:
        cmap = lambda s: s
    nb_x = SSD_WIDTH // BC_WIDTH
    return pl.pallas_call(
        functools.partial(_ssd_kernel, rev=rev),
        grid=(bsz, nc),
        in_specs=[
            pl.BlockSpec((1, SSD_CHUNK, SSD_WIDTH), lambda b, s: (b, cmap(s), 0)),
            pl.BlockSpec((1, SSD_CHUNK, BC_WIDTH), lambda b, s: (b, cmap(s), nb_x)),
            pl.BlockSpec((1, SSD_CHUNK, BC_WIDTH), lambda b, s: (b, cmap(s), nb_x + 1)),
            pl.BlockSpec((1, SSD_CHUNK, LANE), lambda b, s: (b, cmap(s), 0)),
            pl.BlockSpec((1, LANE), lambda b, s: (0, 0)),
            pl.BlockSpec((1, LANE), lambda b, s: (0, 0)),
        ],
        out_specs=pl.BlockSpec((1, SSD_CHUNK, SSD_WIDTH), lambda b, s: (b, cmap(s), 0)),
        out_shape=jax.ShapeDtypeStruct((bsz, nt, SSD_WIDTH), F32),
        scratch_shapes=[pltpu.VMEM((SSD_GROUPS, SSD_STATE, SSD_GROUP_WIDTH), F32)],
        compiler_params=_cparams(("arbitrary", "arbitrary")),
        name="ssd_scan_bwd" if rev else "ssd_scan_fwd",
    )(xbc, xbc, xbc, dt_raw, dt_bias, a_log)


def _mixer_merge_kernel(yf_ref, yb_ref, xs_ref, z_ref, u_ref, v_ref, dsk_ref, sg_ref, vg_ref,
                        ws_ref, bst_ref, o_ref):
    y = yf_ref[...] + yb_ref[...] + xs_ref[...] * dsk_ref[...]
    y_ssd = _norm_rows(y * jax.nn.silu(z_ref[...]), sg_ref[...])
    o_ref[:, :SSD_WIDTH] = y_ssd.astype(o_ref.dtype)
    u = jax.nn.gelu(u_ref[...])
    v = _norm_rows(jax.nn.gelu(v_ref[...]), vg_ref[...]).astype(BF16)
    for g in range(GM_GROUPS):
        sl = slice(g * GM_GROUP_DIM, (g + 1) * GM_GROUP_DIM)
        s = jnp.dot(ws_ref[g], v[:, sl], preferred_element_type=F32) + bst_ref[:, g:g + 1]
        o_ref[:, SSD_WIDTH + g * GM_GROUP_DIM:SSD_WIDTH + (g + 1) * GM_GROUP_DIM] = (u[:, sl] * s).astype(o_ref.dtype)


def _mixer_merge(yf, yb, xbc, proj, d_exp, ssd_g, v_g, w_s, b_s_t):
    rows = yf.shape[0]
    tm = GM_CHUNK
    row = lambda c: (lambda i: (i, c))
    const2 = lambda i: (0, 0)
    return pl.pallas_call(
        _mixer_merge_kernel,
        grid=(rows // tm,),
        in_specs=[
            pl.BlockSpec((tm, SSD_WIDTH), row(0)),
            pl.BlockSpec((tm, SSD_WIDTH), row(0)),
            pl.BlockSpec((tm, SSD_WIDTH), row(0)),
            pl.BlockSpec((tm, SSD_WIDTH), row(0)),
            pl.BlockSpec((tm, GM_WIDTH), row(1)),
            pl.BlockSpec((tm, GM_WIDTH), row(2)),
            pl.BlockSpec((1, SSD_WIDTH), const2),
            pl.BlockSpec((1, SSD_WIDTH), const2),
            pl.BlockSpec((1, GM_WIDTH), const2),
            pl.BlockSpec((GM_GROUPS, GM_CHUNK, GM_CHUNK), lambda i: (0, 0, 0)),
            pl.BlockSpec((GM_CHUNK, GM_GROUPS), const2),
        ],
        out_specs=pl.BlockSpec((tm, SSD_WIDTH + GM_WIDTH), lambda i: (i, 0)),
        out_shape=jax.ShapeDtypeStruct((rows, SSD_WIDTH + GM_WIDTH), BF16),
        compiler_params=_cparams(("arbitrary",)),
        name="mixer_merge",
    )(yf, yb, xbc, proj, proj, proj, d_exp, ssd_g, v_g, w_s, b_s_t)


def _attn_kernel(q_ref, kn_ref, kr_ref, v_ref, o_ref, kcat_ref, *, tk, n_kv):
    @pl.when(pl.program_id(2) == 0)
    def _():
        kcat_ref[:, :QK_NOPE] = kn_ref[0]
        kcat_ref[:, QK_NOPE:] = kr_ref[0]

    q = q_ref[0]
    tq = q.shape[0]

    def body(c, carry):
        m, l, acc = carry
        off = pl.multiple_of(c * tk, tk)
        s = lax.dot_general(q, kcat_ref[pl.ds(off, tk), :], (((1,), (1,)), ((), ())),
                            preferred_element_type=F32)
        m_new = jnp.maximum(m, jnp.max(s, axis=-1, keepdims=True))
        alpha = jnp.exp2(m - m_new)
        p = jnp.exp2(s - m_new)
        l = alpha * l + jnp.sum(p, axis=-1, keepdims=True)
        acc = alpha * acc + jnp.dot(p.astype(BF16), v_ref[0, pl.ds(off, tk), :], preferred_element_type=F32)
        return m_new, l, acc

    init = (jnp.full((tq, 1), -jnp.inf, F32), jnp.zeros((tq, 1), F32), jnp.zeros((tq, V_DIM), F32))
    _, l, acc = lax.fori_loop(0, n_kv, body, init)
    o_ref[0] = (acc / l).astype(o_ref.dtype)


def _attention(q, kv, kr, *, tq, tk):
    bsz, nq, _ = q.shape
    nt = kv.shape[1]
    qk_w = QK_NOPE + 2 * QK_ROPE
    return pl.pallas_call(
        functools.partial(_attn_kernel, tk=tk, n_kv=nt // tk),
        grid=(bsz, MLA_HEADS, nq // tq),
        in_specs=[
            pl.BlockSpec((1, tq, qk_w), lambda b, h, i: (b, i, h)),
            pl.BlockSpec((1, nt, QK_NOPE), lambda b, h, i: (b, 0, 2 * h)),
            pl.BlockSpec((1, nt, 2 * QK_ROPE), lambda b, h, i: (b, 0, 0)),
            pl.BlockSpec((1, nt, V_DIM), lambda b, h, i: (b, 0, 2 * h + 1)),
        ],
        out_specs=pl.BlockSpec((1, tq, V_DIM), lambda b, h, i: (b, i, h)),
        out_shape=jax.ShapeDtypeStruct((bsz, nq, MLA_HEADS * V_DIM), BF16),
        scratch_shapes=[pltpu.VMEM((nt, qk_w), BF16)],
        compiler_params=_cparams(("arbitrary", "arbitrary", "arbitrary")),
        name="mla_attention",
    )(q, kv, kr, kv)


def _ffn_pre_kernel(x_ref, g_ref, mod_ref, rw_ref, rb_ref, h_ref, lg_ref, *, tm, rows_per_batch, n_ctx):
    i = pl.program_id(0)
    is_ctx = _is_ctx_rows(i, tm, rows_per_batch, n_ctx)
    h = _norm_rows(x_ref[...], g_ref[...])
    h = h * (1.0 + _mod_select(mod_ref, 4, is_ctx)) + _mod_select(mod_ref, 3, is_ctx)
    h_ref[...] = h
    lg_ref[...] = jnp.dot(h, rw_ref[...], precision=HIGHEST, preferred_element_type=F32) + rb_ref[...]


def _ffn_pre(x, g, mod, rw_pad, rb_pad, *, tm, rows_per_batch, n_ctx):
    rows, d = x.shape
    batch_of = lambda i: (i * tm) // rows_per_batch
    return pl.pallas_call(
        functools.partial(_ffn_pre_kernel, tm=tm, rows_per_batch=rows_per_batch, n_ctx=n_ctx),
        grid=(rows // tm,),
        in_specs=[
            pl.BlockSpec((tm, d), lambda i: (i, 0)),
            pl.BlockSpec((1, d), lambda i: (0, 0)),
            pl.BlockSpec((1, 2, 6, d), lambda i: (batch_of(i), 0, 0, 0)),
            pl.BlockSpec((d, LANE), lambda i: (0, 0)),
            pl.BlockSpec((1, LANE), lambda i: (0, 0)),
        ],
        out_specs=[pl.BlockSpec((tm, d), lambda i: (i, 0)), pl.BlockSpec((tm, LANE), lambda i: (i, 0))],
        out_shape=[jax.ShapeDtypeStruct((rows, d), F32), jax.ShapeDtypeStruct((rows, LANE), F32)],
        compiler_params=_cparams(("arbitrary",)),
        name="ffn_prenorm_router",
    )(x, g.reshape(1, d), mod, rw_pad, rb_pad)


def _row_copy(src_hbm, row, buf, slot, sem):
    return pltpu.make_async_copy(src_hbm.at[pl.ds(row, 1)], buf.at[pl.ds(slot, 1)], sem)


def _start_rows(src_hbm, ids_ref, n_rows, dst_row, buf, sem):
    def body(r, c):
        _row_copy(src_hbm, ids_ref[0, 0, r], buf, dst_row(r), sem).start()
        return c
    lax.fori_loop(0, n_rows, body, 0, unroll=8)


def _wait_rows(src_hbm, n_rows, buf, sem):
    def body(r, c):
        _row_copy(src_hbm, 0, buf, r, sem).wait()
        return c
    lax.fori_loop(0, n_rows, body, 0, unroll=8)


def _gather_rows_kernel(nv_ref, idx_ref, idx_next_ref, src_hbm, o_ref, buf, sem, *, bm):
    i = pl.program_id(0)
    nv = nv_ref[0]
    cur = i % 2
    same_row = lambda r: r

    @pl.when(i == 0)
    def _():
        _start_rows(src_hbm, idx_ref, bm, same_row, buf.at[0], sem.at[0])

    @pl.when(i + 1 < nv)
    def _():
        _start_rows(src_hbm, idx_next_ref, bm, same_row, buf.at[1 - cur], sem.at[1 - cur])

    @pl.when(i < nv)
    def _():
        _wait_rows(src_hbm, bm, buf.at[cur], sem.at[cur])
        o_ref[...] = buf[cur].astype(o_ref.dtype)

    @pl.when(i >= nv)
    def _():
        o_ref[...] = jnp.zeros_like(o_ref)


def _gather_rows(src, idx, n_valid, *, bm, out_dtype):
    n_slots = idx.shape[0]
    d = src.shape[1]
    nb = n_slots // bm
    clamp = lambda i, nv: jnp.minimum(i, nv[0] - 1)
    idx3 = idx.reshape(nb, 1, bm)
    return pl.pallas_call(
        functools.partial(_gather_rows_kernel, bm=bm),
        grid_spec=pltpu.PrefetchScalarGridSpec(
            num_scalar_prefetch=1,
            grid=(nb,),
            in_specs=[
                pl.BlockSpec((1, 1, bm), lambda i, nv: (clamp(i, nv), 0, 0), memory_space=pltpu.SMEM),
                pl.BlockSpec((1, 1, bm), lambda i, nv: (clamp(i + 1, nv), 0, 0), memory_space=pltpu.SMEM),
                pl.BlockSpec(memory_space=pl.ANY),
            ],
            out_specs=pl.BlockSpec((bm, d), lambda i, nv: (i, 0)),
            scratch_shapes=[pltpu.VMEM((2, bm, d), src.dtype), pltpu.SemaphoreType.DMA((2,))],
        ),
        out_shape=jax.ShapeDtypeStruct((n_slots, d), out_dtype),
        compiler_params=_cparams(("arbitrary",)),
        name="moe_gather_rows",
    )(n_valid, idx3, idx3, src)


DEINT_CHUNK = 2 * LANE


def _deinterleave_kernel(w_ref, p_ref, g_ref, l_ref):
    perm = p_ref[...]
    for c in range(w_ref.shape[1] // DEINT_CHUNK):
        r = jnp.dot(w_ref[:, c * DEINT_CHUNK:(c + 1) * DEINT_CHUNK].astype(BF16), perm, preferred_element_type=F32)
        g_ref[:, c * LANE:(c + 1) * LANE] = r[:, :LANE].astype(BF16)
        l_ref[:, c * LANE:(c + 1) * LANE] = r[:, LANE:].astype(BF16)


def _deinterleave_cast(w1):
    n_e, d, f2 = w1.shape
    rows = n_e * d
    tr = _pick(rows, (512, 256, 128))
    k_i = lax.broadcasted_iota(jnp.int32, (DEINT_CHUNK, DEINT_CHUNK), 0)
    n_i = lax.broadcasted_iota(jnp.int32, (DEINT_CHUNK, DEINT_CHUNK), 1)
    perm = (k_i == jnp.where(n_i < LANE, 2 * n_i, 2 * (n_i - LANE) + 1)).astype(BF16)
    half = jax.ShapeDtypeStruct((rows, f2 // 2), BF16)
    glu, lin = pl.pallas_call(
        _deinterleave_kernel,
        grid=(rows // tr,),
        in_specs=[pl.BlockSpec((tr, f2), lambda i: (i, 0)),
                  pl.BlockSpec((DEINT_CHUNK, DEINT_CHUNK), lambda i: (0, 0))],
        out_specs=[pl.BlockSpec((tr, f2 // 2), lambda i: (i, 0)), pl.BlockSpec((tr, f2 // 2), lambda i: (i, 0))],
        out_shape=[half, half],
        compiler_params=_cparams(("arbitrary",)),
        name="moe_w1_deinterleave",
    )(w1.reshape(rows, f2), perm)
    return glu.reshape(n_e, d, f2 // 2), lin.reshape(n_e, d, f2 // 2)


def _moe_up_kernel(be_ref, nv_ref, x_ref, wg_ref, wl_ref, bg_ref, bl_ref, o_ref):
    @pl.when(pl.program_id(1) < nv_ref[0])
    def _():
        x = x_ref[...]
        a_g = jnp.dot(x, wg_ref[...], preferred_element_type=F32) + bg_ref[...]
        a_l = jnp.dot(x, wl_ref[...], preferred_element_type=F32) + bl_ref[...]
        glu = jnp.minimum(a_g, SWIGLU_LIMIT)
        lin = jnp.clip(a_l, -SWIGLU_LIMIT, SWIGLU_LIMIT)
        o_ref[...] = (glu * jax.nn.sigmoid(SWIGLU_ALPHA * glu) * (lin + 1.0)).astype(o_ref.dtype)

    @pl.when(pl.program_id(1) >= nv_ref[0])
    def _():
        o_ref[...] = jnp.zeros_like(o_ref)


def _moe_up(xs, w_glu, w_lin, b_glu, b_lin, block_expert, n_valid, *, tn):
    n_slots, d = xs.shape
    ff = w_glu.shape[2]
    nb = n_slots // MOE_ROWS
    blk = lambda i, nv: jnp.minimum(i, nv[0] - 1)
    wmap = lambda j, i, be, nv: (be[blk(i, nv)], 0, j)
    return pl.pallas_call(
        _moe_up_kernel,
        grid_spec=pltpu.PrefetchScalarGridSpec(
            num_scalar_prefetch=2,
            grid=(ff // tn, nb),
            in_specs=[
                pl.BlockSpec((MOE_ROWS, d), lambda j, i, be, nv: (blk(i, nv), 0)),
                pl.BlockSpec((None, d, tn), wmap),
                pl.BlockSpec((None, d, tn), wmap),
                pl.BlockSpec((None, 1, tn), wmap),
                pl.BlockSpec((None, 1, tn), wmap),
            ],
            out_specs=pl.BlockSpec((MOE_ROWS, tn), lambda j, i, be, nv: (i, j)),
        ),
        out_shape=jax.ShapeDtypeStruct((n_slots, ff), BF16),
        compiler_params=_cparams(("arbitrary", "arbitrary")),
        name="moe_up",
    )(block_expert, n_valid, xs, w_glu, w_lin, b_glu, b_lin)


def _moe_down_kernel(be_ref, nv_ref, a_ref, w_ref, b_ref, o_ref):
    @pl.when(pl.program_id(1) < nv_ref[0])
    def _():
        o_ref[...] = jnp.dot(a_ref[...], w_ref[...], preferred_element_type=F32) + b_ref[...]

    @pl.when(pl.program_id(1) >= nv_ref[0])
    def _():
        o_ref[...] = jnp.zeros_like(o_ref)


def _moe_down(act, w2, b2, block_expert, n_valid, *, tn):
    n_slots, ff = act.shape
    d = w2.shape[2]
    nb = n_slots // MOE_ROWS
    blk = lambda i, nv: jnp.minimum(i, nv[0] - 1)
    wmap = lambda j, i, be, nv: (be[blk(i, nv)], 0, j)
    return pl.pallas_call(
        _moe_down_kernel,
        grid_spec=pltpu.PrefetchScalarGridSpec(
            num_scalar_prefetch=2,
            grid=(d // tn, nb),
            in_specs=[
                pl.BlockSpec((MOE_ROWS, ff), lambda j, i, be, nv: (blk(i, nv), 0)),
                pl.BlockSpec((None, ff, tn), wmap),
                pl.BlockSpec((None, 1, tn), wmap),
            ],
            out_specs=pl.BlockSpec((MOE_ROWS, tn), lambda j, i, be, nv: (i, j)),
        ),
        out_shape=jax.ShapeDtypeStruct((n_slots, d), F32),
        compiler_params=_cparams(("arbitrary", "arbitrary")),
        name="moe_down",
    )(block_expert, n_valid, act, w2, b2)


def _moe_combine_kernel(pos_ref, pos_next_ref, y_hbm, gate_ref, x_ref, mod_ref, fg_ref, o_ref, buf, sem, *,
                        tm, rows_per_batch, n_ctx, final_norm):
    n_rows = TOP_K * tm
    i = pl.program_id(0)
    cur = i % 2
    choice_major = lambda r: (r % TOP_K) * tm + r // TOP_K

    @pl.when(i == 0)
    def _():
        _start_rows(y_hbm, pos_ref, n_rows, choice_major, buf.at[0], sem.at[0])

    @pl.when(i + 1 < pl.num_programs(0))
    def _():
        _start_rows(y_hbm, pos_next_ref, n_rows, choice_major, buf.at[1 - cur], sem.at[1 - cur])

    _wait_rows(y_hbm, n_rows, buf.at[cur], sem.at[cur])

    gates = gate_ref[...]
    f = gates[:, 0:1] * buf[cur, 0:tm, :]
    for kk in range(1, TOP_K):
        f = f + gates[:, kk:kk + 1] * buf[cur, kk * tm:(kk + 1) * tm, :]
    is_ctx = _is_ctx_rows(i, tm, rows_per_batch, n_ctx)
    out = x_ref[...] + _mod_select(mod_ref, 5, is_ctx) * f
    if final_norm:
        out = _norm_rows(out, fg_ref[...])
    o_ref[...] = out


def _moe_combine(y2, pos, gates, x, mod, final_g, *, tm, rows_per_batch, n_ctx, final_norm):
    rows, d = x.shape
    nblk = rows // tm
    batch_of = lambda i: (i * tm) // rows_per_batch
    pos3 = pos.reshape(nblk, 1, TOP_K * tm)
    return pl.pallas_call(
        functools.partial(_moe_combine_kernel, tm=tm, rows_per_batch=rows_per_batch, n_ctx=n_ctx,
                          final_norm=final_norm),
        grid=(nblk,),
        in_specs=[
            pl.BlockSpec((1, 1, TOP_K * tm), lambda i: (i, 0, 0), memory_space=pltpu.SMEM),
            pl.BlockSpec((1, 1, TOP_K * tm), lambda i: (jnp.minimum(i + 1, nblk - 1), 0, 0),
                         memory_space=pltpu.SMEM),
            pl.BlockSpec(memory_space=pl.ANY),
            pl.BlockSpec((tm, TOP_K), lambda i: (i, 0)),
            pl.BlockSpec((tm, d), lambda i: (i, 0)),
            pl.BlockSpec((1, 2, 6, d), lambda i: (batch_of(i), 0, 0, 0)),
            pl.BlockSpec((1, d), lambda i: (0, 0)),
        ],
        out_specs=pl.BlockSpec((tm, d), lambda i: (i, 0)),
        out_shape=jax.ShapeDtypeStruct((rows, d), F32),
        scratch_shapes=[pltpu.VMEM((2, TOP_K * tm, d), F32), pltpu.SemaphoreType.DMA((2,))],
        compiler_params=_cparams(("arbitrary",)),
        name="moe_combine",
    )(pos3, pos3, y2, gates, x, mod, final_g.reshape(1, d))


def _route(logits, n_experts):
    t_tok = logits.shape[0]
    top_val, top_idx = lax.top_k(logits, TOP_K)
    gates = jax.nn.softmax(top_val, axis=-1)
    flat_e = top_idx.reshape(-1).astype(jnp.int32)
    n_assign = t_tok * TOP_K
    experts = jnp.arange(n_experts, dtype=jnp.int32)
    counts = jnp.sum((flat_e[:, None] == experts[None, :]).astype(jnp.int32), axis=0)
    padded = (counts + MOE_ROWS - 1) // MOE_ROWS * MOE_ROWS
    incl = (experts[None, :] <= experts[:, None]).astype(jnp.int32)
    cum_pad = jnp.sum(incl * padded[None, :], axis=1)
    pad_start = cum_pad - padded
    start = jnp.sum(incl * counts[None, :], axis=1) - counts
    order = jnp.argsort(flat_e).astype(jnp.int32)
    rank = jnp.argsort(order).astype(jnp.int32)
    pos = pad_start[flat_e] + rank - start[flat_e]
    n_blocks = n_assign // MOE_ROWS + n_experts
    block_start = jnp.arange(n_blocks, dtype=jnp.int32) * MOE_ROWS
    block_expert = jnp.clip(jnp.sum((cum_pad[None, :] <= block_start[:, None]).astype(jnp.int32), axis=1),
                            0, n_experts - 1)
    slot_e = jnp.repeat(block_expert, MOE_ROWS)
    within = jnp.arange(n_blocks * MOE_ROWS, dtype=jnp.int32) - pad_start[slot_e]
    src = jnp.clip(start[slot_e] + within, 0, n_assign - 1)
    slot_tok = jnp.where(within < counts[slot_e], order[src] // TOP_K, 0)
    n_valid = cum_pad[-1:] // MOE_ROWS
    return gates, slot_tok, pos, block_expert, n_valid


def _moe_layer(x, norm_g, mod, router_w, router_b, w1, b1, w2, b2, final_g, *,
               rows_per_batch, n_ctx, tm_rows, final_norm):
    rows, d = x.shape
    n_experts = router_w.shape[1]
    ff = w2.shape[1]
    rw_pad = jnp.pad(router_w, ((0, 0), (0, LANE - n_experts)))
    rb_pad = jnp.pad(router_b, (0, LANE - n_experts)).reshape(1, LANE)
    hf, logits = _ffn_pre(x, norm_g, mod, rw_pad, rb_pad, tm=tm_rows, rows_per_batch=rows_per_batch, n_ctx=n_ctx)
    gates, slot_tok, pos, block_expert, n_valid = _route(logits[:, :n_experts], n_experts)
    xs = _gather_rows(hf, slot_tok, n_valid, bm=MOE_ROWS, out_dtype=BF16)
    w_glu, w_lin = _deinterleave_cast(w1)
    b_glu = b1[:, 0::2].reshape(n_experts, 1, ff)
    b_lin = b1[:, 1::2].reshape(n_experts, 1, ff)
    act = _moe_up(xs, w_glu, w_lin, b_glu, b_lin, block_expert, n_valid, tn=_pick(ff, (2048, 1024, 512, 256, 128)))
    y2 = _moe_down(act, w2.astype(BF16), b2.reshape(n_experts, 1, d), block_expert, n_valid,
                   tn=_pick(d, (2048, 1024, 512, 256, 128)))
    tm_c = _pick(rows_per_batch, (128,))
    return _moe_combine(y2, pos, gates, x, mod, final_g, tm=tm_c, rows_per_batch=rows_per_batch,
                        n_ctx=n_ctx, final_norm=final_norm)


def _modulation(c, c_ctx, ada_w_i, ada_b_i):
    bsz, d = c.shape
    rows = 16
    s = jnp.concatenate([jax.nn.silu(c), jax.nn.silu(c_ctx)[None], jnp.zeros((rows - bsz - 1, d), F32)], axis=0)
    n = ada_w_i.shape[1]
    m = _fused_matmul(s, ada_w_i, name="adaln_modulation", m_rows=rows, k=d, tm=rows, tn=_pick(n, (1024, 512, 256, 128)), out_dtype=F32,
                      rows_per_batch=rows, epilogue="bias", bias=ada_b_i)
    lat = m[:bsz].reshape(bsz, 6, d)
    ctx = jnp.broadcast_to(m[bsz].reshape(1, 6, d), (bsz, 6, d))
    return jnp.stack([ctx, lat], axis=1)


def _dwconv_silu(xbc, conv_w, conv_b, n_ctx):
    nt = xbc.shape[1]
    t = jnp.arange(nt)[None, :, None]
    prev = jnp.pad(xbc, ((0, 0), (1, 0), (0, 0)))[:, :nt]
    nxt = jnp.pad(xbc, ((0, 0), (0, 1), (0, 0)))[:, 1:]
    prev = jnp.where((t == 0) | (t == n_ctx), 0.0, prev)
    nxt = jnp.where((t == n_ctx - 1) | (t == nt - 1), 0.0, nxt)
    out = conv_b + prev * conv_w[0] + xbc * conv_w[1] + nxt * conv_w[2]
    return jax.nn.silu(out)


def _rope_tables(n_lat):
    rows = n_lat // GRID_W
    row = jnp.broadcast_to(jnp.arange(rows)[:, None], (rows, GRID_W)).reshape(-1)
    col = jnp.broadcast_to(jnp.arange(GRID_W)[None, :], (rows, GRID_W)).reshape(-1)
    freqs = ROPE_THETA ** (-jnp.arange(ROPE_PAIRS, dtype=F32) / ROPE_PAIRS)
    ang = jnp.stack([row[:, None] * freqs, col[:, None] * freqs], axis=1)
    cos, sin = jnp.cos(ang).astype(F32), jnp.sin(ang).astype(F32)
    c_tab = jnp.concatenate([cos[:, 0], cos[:, 0], cos[:, 1], cos[:, 1]], axis=-1)
    s_tab = jnp.concatenate([-sin[:, 0], sin[:, 0], -sin[:, 1], sin[:, 1]], axis=-1)
    return c_tab, s_tab


def _swap16(w):
    lane = jnp.arange(w.shape[-1])
    return jnp.take(w, jnp.where(lane % 32 < 16, lane + 16, lane - 16), axis=-1)


def kernel(x, c, ctx, c_ctx, mix_norm_g, ffn_norm_g, ada_w, ada_b, hyb_w_in, hyb_conv_w, hyb_conv_b, hyb_dt_bias, hyb_a_log, hyb_d_skip, hyb_ssd_norm_g, hyb_v_norm_g, hyb_w_s, hyb_b_s, hyb_w_out, mla_w_in, mla_q_norm_g, mla_kv_norm_g, mla_w_uq, mla_w_ukv, mla_w_o, router_w, router_b, exp_w1, exp_b1, exp_w2, exp_b2, final_norm_g):
    bsz, n_lat, d = x.shape
    n_ctx = ctx.shape[1]
    nt = n_ctx + n_lat
    rows = bsz * nt
    tm_s = _pick(nt, (768, 384, 256, 128))
    tm_l = _pick(n_lat, (1024, 512, 256, 128))

    stream = jnp.concatenate([ctx, x], axis=1).reshape(rows, d)

    mod0 = _modulation(c, c_ctx, ada_w[0], ada_b[0])
    cuts = (SSD_WIDTH, SSD_WIDTH + XBC_WIDTH, SSD_WIDTH + XBC_WIDTH + 2 * SSD_HEADS)
    w_in = hyb_w_in[0]
    w_main = jnp.concatenate([w_in[:, :cuts[0]], w_in[:, cuts[2]:], w_in[:, cuts[0]:cuts[1]]], axis=1).astype(BF16)
    w_dt = jnp.pad(w_in[:, cuts[1]:cuts[2]], ((0, 0), (0, LANE - 2 * SSD_HEADS))).astype(BF16)
    pro = dict(prologue="norm_mod", g=mix_norm_g[0], mod=mod0, sh_idx=0, sc_idx=1,
               rows_per_batch=nt, n_ctx=n_ctx, m_rows=rows, k=d, tm=tm_s)
    proj = _fused_matmul(stream, w_main, name="hyb_in_proj", tn=_pick(w_main.shape[1], (1024, 512, 256, 128)), out_dtype=F32, **pro)
    dt_raw = _fused_matmul(stream, w_dt, name="hyb_dt_proj", tn=LANE, out_dtype=F32, **pro)

    xbc_off = SSD_WIDTH + 2 * GM_WIDTH
    xbc = _dwconv_silu(proj[:, xbc_off:].reshape(bsz, nt, XBC_WIDTH), hyb_conv_w[0], hyb_conv_b[0], n_ctx)
    dt_bias = jnp.pad(hyb_dt_bias[0].reshape(1, -1), ((0, 0), (0, LANE - 2 * SSD_HEADS)))
    a_log = jnp.pad(hyb_a_log[0].reshape(1, -1), ((0, 0), (0, LANE - 2 * SSD_HEADS)))
    dt3 = dt_raw.reshape(bsz, nt, LANE)
    n_ctx_chunks = n_ctx // SSD_CHUNK
    y_f = _ssd_scan(xbc, dt3, dt_bias, a_log, rev=False, n_ctx_chunks=n_ctx_chunks)
    y_b = _ssd_scan(xbc, dt3, dt_bias, a_log, rev=True, n_ctx_chunks=n_ctx_chunks)

    d_exp = jnp.repeat(hyb_d_skip[0], SSD_HEAD_DIM).reshape(1, SSD_WIDTH)
    y_cat = _mixer_merge(y_f.reshape(rows, SSD_WIDTH), y_b.reshape(rows, SSD_WIDTH),
                         xbc.reshape(rows, XBC_WIDTH), proj, d_exp,
                         hyb_ssd_norm_g[0].reshape(1, -1), hyb_v_norm_g[0].reshape(1, -1),
                         hyb_w_s[0].astype(BF16), hyb_b_s[0].T)
    stream = _fused_matmul(y_cat, hyb_w_out[0].astype(BF16), name="hyb_out_proj", m_rows=rows, k=SSD_WIDTH + GM_WIDTH, tm=tm_s,
                           tn=_pick(d, (1024, 512, 256, 128)), out_dtype=F32, rows_per_batch=nt, n_ctx=n_ctx,
                           epilogue="gated_resid", resid=stream, mod=mod0, gate_idx=2)
    stream = _moe_layer(stream, ffn_norm_g[0], mod0, router_w[0], router_b[0], exp_w1[0], exp_b1[0],
                        exp_w2[0], exp_b2[0], final_norm_g, rows_per_batch=nt, n_ctx=n_ctx, tm_rows=tm_s,
                        final_norm=False)

    mod1 = _modulation(c, c_ctx, ada_w[1], ada_b[1])
    w_in1 = mla_w_in[0]
    kpe_w = w_in1[:, Q_LORA + KV_LORA:]
    w_in1_r = jnp.concatenate([w_in1[:, Q_LORA:Q_LORA + KV_LORA], kpe_w, _swap16(kpe_w),
                               jnp.zeros((d, Q_LORA - KV_LORA - 2 * QK_ROPE), F32), w_in1[:, :Q_LORA]],
                              axis=1).astype(BF16)
    p1 = _fused_matmul(stream, w_in1_r, name="mla_in_proj", m_rows=rows, k=d, tm=tm_s, tn=w_in1_r.shape[1], out_dtype=F32,
                       prologue="norm_mod", g=mix_norm_g[1], mod=mod1, sh_idx=0, sc_idx=1,
                       rows_per_batch=nt, n_ctx=n_ctx)

    c_tab, s_tab = _rope_tables(n_lat)
    ck = jnp.concatenate([jnp.ones((n_ctx, QK_ROPE), F32), c_tab], axis=0)
    sk = jnp.concatenate([jnp.zeros((n_ctx, QK_ROPE), F32), s_tab], axis=0)
    p1_3 = p1.reshape(bsz, nt, -1)
    k_rot = p1_3[:, :, KV_LORA:KV_LORA + QK_ROPE] * ck + p1_3[:, :, KV_LORA + QK_ROPE:KV_LORA + 2 * QK_ROPE] * sk
    kr = jnp.concatenate([k_rot, k_rot], axis=-1).astype(BF16)

    kv = _fused_matmul(p1, mla_w_ukv[0].astype(BF16), name="mla_kv_up", m_rows=rows, k=KV_LORA, tm=tm_s,
                       tn=_pick(mla_w_ukv.shape[2], (2048, 1024, 512, 256, 128)), out_dtype=BF16,
                       rows_per_batch=nt, prologue="norm", g=mla_kv_norm_g[0], x_col_block=0)

    w_uq = mla_w_uq[0].reshape(Q_LORA, MLA_HEADS, QK_NOPE + QK_ROPE)
    w_q = jnp.concatenate([w_uq, _swap16(w_uq[:, :, QK_NOPE:])], axis=-1)
    w_q = w_q.reshape(Q_LORA, MLA_HEADS * (QK_NOPE + 2 * QK_ROPE)).astype(BF16)
    tm_q = _pick(n_ctx, (256, 128))
    lat_blocks = n_lat // tm_q
    q_rows = lambda i: (i // lat_blocks) * (nt // tm_q) + n_ctx // tm_q + i % lat_blocks
    q_tab = jnp.concatenate([jnp.ones((n_lat, QK_NOPE), F32), c_tab, s_tab], axis=-1) * (MLA_SCALE * math.log2(math.e))
    q = _fused_matmul(p1, w_q, name="mla_q_proj", m_rows=bsz * n_lat, k=Q_LORA, tm=tm_q, tn=w_q.shape[1],
                      out_dtype=BF16, rows_per_batch=n_lat, prologue="norm", g=mla_q_norm_g[0],
                      x_row_map=q_rows, x_col_block=1, epilogue="mul_table", table=q_tab,
                      table_row_map=lambda i: i % lat_blocks)

    o = _attention(q.reshape(bsz, n_lat, -1), kv.reshape(bsz, nt, -1), kr,
                   tq=_pick(n_lat, (1024, 512, 256, 128)), tk=_pick(nt, (2816, 768, 256, 128)))

    lat = stream.reshape(bsz, nt, d)[:, n_ctx:].reshape(bsz * n_lat, d)
    mod1_lat = mod1
    lat = _fused_matmul(o.reshape(bsz * n_lat, -1), mla_w_o[0].astype(BF16), name="mla_out_proj",
                        m_rows=bsz * n_lat,
                        k=MLA_HEADS * V_DIM, tm=tm_l, tn=_pick(d, (1024, 512, 256, 128)), out_dtype=F32,
                        rows_per_batch=n_lat, epilogue="gated_resid", resid=lat, mod=mod1_lat, gate_idx=2)
    out = _moe_layer(lat, ffn_norm_g[1], mod1_lat, router_w[1], router_b[1], exp_w1[1], exp_b1[1],
                     exp_w2[1], exp_b2[1], final_norm_g, rows_per_batch=n_lat, n_ctx=0, tm_rows=tm_l,
                     final_norm=True)
    return out.reshape(bsz, n_lat, d)
```

```python
import functools
import math

import jax
import jax.numpy as jnp
from jax import lax
from jax.experimental import pallas as pl
from jax.experimental.pallas import tpu as pltpu

F32 = jnp.float32
BF16 = jnp.bfloat16
HIGHEST = lax.Precision.HIGHEST
EPS = 1e-6

GRID_W = 64
SSD_HEADS = 32
SSD_HEAD_DIM = 64
SSD_WIDTH = SSD_HEADS * SSD_HEAD_DIM
SSD_GROUPS = 4
SSD_HEADS_PER_GROUP = SSD_HEADS // SSD_GROUPS
SSD_STATE = 128
SSD_CHUNK = 128
SSD_GROUP_WIDTH = SSD_HEADS_PER_GROUP * SSD_HEAD_DIM
BC_WIDTH = SSD_GROUPS * SSD_STATE
XBC_WIDTH = SSD_WIDTH + 2 * BC_WIDTH
GM_GROUPS = 16
GM_GROUP_DIM = 128
GM_WIDTH = GM_GROUPS * GM_GROUP_DIM
GM_CHUNK = 128
MLA_HEADS = 16
Q_LORA = 768
KV_LORA = 512
QK_NOPE = 128
QK_ROPE = 64
V_DIM = 128
ROPE_PAIRS = QK_ROPE // 4
ROPE_THETA = 10000.0
MLA_SCALE = (QK_NOPE + QK_ROPE) ** -0.5
TOP_K = 4
SWIGLU_LIMIT = 7.0
SWIGLU_ALPHA = 1.702
MOE_ROWS = 256

LANE = 128
DMA_QUEUES = 2
VMEM_LIMIT = 56 * 1024 * 1024


def _pick(n, candidates):
    for c in candidates:
        if n % c == 0:
            return c
    raise ValueError(f"no tile for {n} in {candidates}")


def _cparams(sem):
    return pltpu.CompilerParams(dimension_semantics=sem, vmem_limit_bytes=VMEM_LIMIT)


def _is_ctx_rows(i, tm, rows_per_batch, n_ctx):
    row = (i * tm) % rows_per_batch + lax.broadcasted_iota(jnp.int32, (tm, 1), 0)
    return row < n_ctx


def _mod_select(mod_ref, idx, is_ctx):
    return jnp.where(is_ctx, mod_ref[0, 0, idx:idx + 1, :], mod_ref[0, 1, idx:idx + 1, :])


def _norm_rows(x, g):
    y = x * lax.rsqrt(jnp.mean(x * x, axis=-1, keepdims=True) + EPS)
    return y * g


def _fused_matmul_kernel(*refs, prologue, epilogue, tm, tn, rows_per_batch, n_ctx,
                         sh_idx, sc_idx, gate_idx, use_scratch):
    it = iter(refs)
    x_ref = next(it)
    g_ref = next(it) if prologue != "none" else None
    pmod_ref = next(it) if prologue == "norm_mod" else None
    w_ref = next(it)
    bias_ref = next(it) if epilogue == "bias" else None
    resid_ref = next(it) if epilogue == "gated_resid" else None
    emod_ref = next(it) if epilogue == "gated_resid" else None
    tab_ref = next(it) if epilogue == "mul_table" else None
    o_ref = next(it)
    xn_ref = next(it) if use_scratch else None

    i = pl.program_id(0)
    j = pl.program_id(1)

    if use_scratch:
        @pl.when(j == 0)
        def _():
            x = x_ref[...].astype(F32)
            if prologue != "none":
                x = _norm_rows(x, g_ref[...])
            if prologue == "norm_mod":
                is_ctx = _is_ctx_rows(i, tm, rows_per_batch, n_ctx)
                x = x * (1.0 + _mod_select(pmod_ref, sc_idx, is_ctx)) + _mod_select(pmod_ref, sh_idx, is_ctx)
            xn_ref[...] = x.astype(BF16)
        xn = xn_ref[...]
    else:
        xn = x_ref[...]

    acc = jnp.dot(xn, w_ref[...].astype(BF16), preferred_element_type=F32)
    if epilogue == "bias":
        acc = acc + bias_ref[...]
    elif epilogue == "gated_resid":
        is_ctx = _is_ctx_rows(i, tm, rows_per_batch, n_ctx)
        acc = resid_ref[...] + _mod_select(emod_ref, gate_idx, is_ctx) * acc
    elif epilogue == "mul_table":
        acc = acc * jnp.tile(tab_ref[...], (1, tn // tab_ref.shape[1]))
    o_ref[...] = acc.astype(o_ref.dtype)


def _fused_matmul(x, w, *, name, m_rows, k, tm, tn, out_dtype, rows_per_batch, n_ctx=0,
                  prologue="none", g=None, mod=None, sh_idx=0, sc_idx=0,
                  epilogue="none", bias=None, resid=None, gate_idx=0, table=None,
                  x_row_map=None, x_col_block=0, w_row_block=0, resid_row_map=None, table_row_map=None):
    n = w.shape[1]
    assert w.shape[0] % k == 0 and m_rows % tm == 0 and n % tn == 0 and rows_per_batch % tm == 0
    ident = lambda i: i
    x_row_map = x_row_map or ident
    resid_row_map = resid_row_map or ident
    table_row_map = table_row_map or ident
    batch_of = lambda i: (i * tm) // rows_per_batch
    use_scratch = prologue != "none" or x.dtype != BF16

    args = [x]
    in_specs = [pl.BlockSpec((tm, k), lambda i, j: (x_row_map(i), x_col_block))]
    if prologue != "none":
        args.append(g.reshape(1, k))
        in_specs.append(pl.BlockSpec((1, k), lambda i, j: (0, 0)))
    if prologue == "norm_mod":
        args.append(mod)
        in_specs.append(pl.BlockSpec((1, 2, 6, k), lambda i, j: (batch_of(i), 0, 0, 0)))
    args.append(w)
    in_specs.append(pl.BlockSpec((k, tn), lambda i, j: (w_row_block, j)))
    if epilogue == "bias":
        args.append(bias.reshape(1, n))
        in_specs.append(pl.BlockSpec((1, tn), lambda i, j: (0, j)))
    if epilogue == "gated_resid":
        args.append(resid)
        in_specs.append(pl.BlockSpec((tm, tn), lambda i, j: (resid_row_map(i), j)))
        args.append(mod)
        in_specs.append(pl.BlockSpec((1, 2, 6, tn), lambda i, j: (batch_of(i), 0, 0, j)))
    if epilogue == "mul_table":
        args.append(table)
        in_specs.append(pl.BlockSpec((tm, table.shape[1]), lambda i, j: (table_row_map(i), 0)))

    kern = functools.partial(
        _fused_matmul_kernel, prologue=prologue, epilogue=epilogue, tm=tm, tn=tn,
        rows_per_batch=rows_per_batch, n_ctx=n_ctx, sh_idx=sh_idx, sc_idx=sc_idx,
        gate_idx=gate_idx, use_scratch=use_scratch)
    return pl.pallas_call(
        kern,
        grid=(m_rows // tm, n // tn),
        in_specs=in_specs,
        out_specs=pl.BlockSpec((tm, tn), lambda i, j: (i, j)),
        out_shape=jax.ShapeDtypeStruct((m_rows, n), out_dtype),
        scratch_shapes=[pltpu.VMEM((tm, k), BF16)] if use_scratch else [],
        compiler_params=_cparams(("arbitrary", "arbitrary")),
        name=name,
    )(*args)


def _expand_heads(mat, col0):
    rows = mat.shape[0]
    return jnp.concatenate(
        [jnp.broadcast_to(mat[:, col0 + jj:col0 + jj + 1], (rows, SSD_HEAD_DIM))
         for jj in range(SSD_HEADS_PER_GROUP)], axis=1)


def _ssd_kernel(xs_ref, b_ref, c_ref, dtr_ref, dtb_ref, alog_ref, y_ref, h_ref, *, rev):
    L = SSD_CHUNK
    step = pl.program_id(1)

    @pl.when(step == 0)
    def _():
        h_ref[...] = jnp.zeros_like(h_ref)

    dcol = SSD_HEADS if rev else 0
    dt = jax.nn.softplus(dtr_ref[0] + dtb_ref[...])
    a = dt * (-jnp.exp(alog_ref[...]))
    r_i = lax.broadcasted_iota(jnp.int32, (L, L), 0)
    c_i = lax.broadcasted_iota(jnp.int32, (L, L), 1)
    tri = (r_i <= c_i) if rev else (r_i >= c_i)
    acs = jnp.dot(tri.astype(F32), a, precision=HIGHEST, preferred_element_type=F32)
    acs_t = acs.T
    edge = acs[0:1, :] if rev else acs[L - 1:L, :]
    decay_end = jnp.exp(edge - acs)
    exp_acs = jnp.exp(acs)
    chunk_decay = jnp.exp(edge)

    for g in range(SSD_GROUPS):
        col0 = dcol + g * SSD_HEADS_PER_GROUP
        bg = b_ref[0, :, g * SSD_STATE:(g + 1) * SSD_STATE]
        cg = c_ref[0, :, g * SSD_STATE:(g + 1) * SSD_STATE].astype(BF16)
        cb = lax.dot_general(cg, bg.astype(BF16), (((1,), (1,)), ((), ())), preferred_element_type=F32)
        xdt = xs_ref[0, :, g * SSD_GROUP_WIDTH:(g + 1) * SSD_GROUP_WIDTH] * _expand_heads(dt, col0)
        y_diag = []
        for jj in range(SSD_HEADS_PER_GROUP):
            col = col0 + jj
            seg = acs[:, col:col + 1] - acs_t[col:col + 1, :]
            lmat = jnp.exp(jnp.where(tri, seg, -jnp.inf))
            y_diag.append(jnp.dot((cb * lmat).astype(BF16),
                                  xdt[:, jj * SSD_HEAD_DIM:(jj + 1) * SSD_HEAD_DIM].astype(BF16),
                                  preferred_element_type=F32))
        y_diag = jnp.concatenate(y_diag, axis=1)
        h_g = h_ref[g]
        y_off = jnp.dot(cg, h_g.astype(BF16), preferred_element_type=F32) * _expand_heads(exp_acs, col0)
        y_ref[0, :, g * SSD_GROUP_WIDTH:(g + 1) * SSD_GROUP_WIDTH] = y_diag + y_off
        states = jnp.dot(bg.T.astype(BF16), (xdt * _expand_heads(decay_end, col0)).astype(BF16),
                         preferred_element_type=F32)
        h_ref[g] = h_g * _expand_heads(chunk_decay, col0) + states


def _ssd_scan(xbc, dt_raw, dt_bias, a_log, *, rev, n_ctx_chunks):
    bsz, nt, _ = xbc.shape
    nc = nt // SSD_CHUNK
    if rev:
        cmap = lambda s: jnp.where(s < n_ctx_chunks, n_ctx_chunks - 1 - s, nc - 1 + n_ctx_chunks - s)
    else:
        cmap = lambda s: s
    nb_x = SSD_WIDTH // BC_WIDTH
    return pl.pallas_call(
        functools.partial(_ssd_kernel, rev=rev),
        grid=(bsz, nc),
        in_specs=[
            pl.BlockSpec((1, SSD_CHUNK, SSD_WIDTH), lambda b, s: (b, cmap(s), 0)),
            pl.BlockSpec((1, SSD_CHUNK, BC_WIDTH), lambda b, s: (b, cmap(s), nb_x)),
            pl.BlockSpec((1, SSD_CHUNK, BC_WIDTH), lambda b, s: (b, cmap(s), nb_x + 1)),
            pl.BlockSpec((1, SSD_CHUNK, LANE), lambda b, s: (b, cmap(s), 0)),
            pl.BlockSpec((1, LANE), lambda b, s: (0, 0)),
            pl.BlockSpec((1, LANE), lambda b, s: (0, 0)),
        ],
        out_specs=pl.BlockSpec((1, SSD_CHUNK, SSD_WIDTH), lambda b, s: (b, cmap(s), 0)),
        out_shape=jax.ShapeDtypeStruct((bsz, nt, SSD_WIDTH), F32),
        scratch_shapes=[pltpu.VMEM((SSD_GROUPS, SSD_STATE, SSD_GROUP_WIDTH), F32)],
        compiler_params=_cparams(("arbitrary", "arbitrary")),
        name="ssd_scan_bwd" if rev else "ssd_scan_fwd",
    )(xbc, xbc, xbc, dt_raw, dt_bias, a_log)


def _mixer_merge_kernel(yf_ref, yb_ref, xs_ref, z_ref, u_ref, v_ref, dsk_ref, sg_ref, vg_ref,
                        ws_ref, bst_ref, o_ref):
    y = yf_ref[...] + yb_ref[...] + xs_ref[...] * dsk_ref[...]
    y_ssd = _norm_rows(y * jax.nn.silu(z_ref[...]), sg_ref[...])
    o_ref[:, :SSD_WIDTH] = y_ssd.astype(o_ref.dtype)
    u = jax.nn.gelu(u_ref[...])
    v = _norm_rows(jax.nn.gelu(v_ref[...]), vg_ref[...]).astype(BF16)
    for g in range(GM_GROUPS):
        sl = slice(g * GM_GROUP_DIM, (g + 1) * GM_GROUP_DIM)
        s = jnp.dot(ws_ref[g], v[:, sl], preferred_element_type=F32) + bst_ref[:, g:g + 1]
        o_ref[:, SSD_WIDTH + g * GM_GROUP_DIM:SSD_WIDTH + (g + 1) * GM_GROUP_DIM] = (u[:, sl] * s).astype(o_ref.dtype)


def _mixer_merge(yf, yb, xbc, proj, d_exp, ssd_g, v_g, w_s, b_s_t):
    rows = yf.shape[0]
    tm = GM_CHUNK
    row = lambda c: (lambda i: (i, c))
    const2 = lambda i: (0, 0)
    return pl.pallas_call(
        _mixer_merge_kernel,
        grid=(rows // tm,),
        in_specs=[
            pl.BlockSpec((tm, SSD_WIDTH), row(0)),
            pl.BlockSpec((tm, SSD_WIDTH), row(0)),
            pl.BlockSpec((tm, SSD_WIDTH), row(0)),
            pl.BlockSpec((tm, SSD_WIDTH), row(0)),
            pl.BlockSpec((tm, GM_WIDTH), row(1)),
            pl.BlockSpec((tm, GM_WIDTH), row(2)),
            pl.BlockSpec((1, SSD_WIDTH), const2),
            pl.BlockSpec((1, SSD_WIDTH), const2),
            pl.BlockSpec((1, GM_WIDTH), const2),
            pl.BlockSpec((GM_GROUPS, GM_CHUNK, GM_CHUNK), lambda i: (0, 0, 0)),
            pl.BlockSpec((GM_CHUNK, GM_GROUPS), const2),
        ],
        out_specs=pl.BlockSpec((tm, SSD_WIDTH + GM_WIDTH), lambda i: (i, 0)),
        out_shape=jax.ShapeDtypeStruct((rows, SSD_WIDTH + GM_WIDTH), BF16),
        compiler_params=_cparams(("arbitrary",)),
        name="mixer_merge",
    )(yf, yb, xbc, proj, proj, proj, d_exp, ssd_g, v_g, w_s, b_s_t)


def _attn_kernel(q_ref, kn_ref, kr_ref, v_ref, o_ref, kcat_ref, *, tk, n_kv):
    @pl.when(pl.program_id(2) == 0)
    def _():
        kcat_ref[:, :QK_NOPE] = kn_ref[0]
        kcat_ref[:, QK_NOPE:] = kr_ref[0]

    q = q_ref[0]
    tq = q.shape[0]

    def body(c, carry):
        m, l, acc = carry
        off = pl.multiple_of(c * tk, tk)
        s = lax.dot_general(q, kcat_ref[pl.ds(off, tk), :], (((1,), (1,)), ((), ())),
                            preferred_element_type=F32)
        m_new = jnp.maximum(m, jnp.max(s, axis=-1, keepdims=True))
        alpha = jnp.exp2(m - m_new)
        p = jnp.exp2(s - m_new)
        l = alpha * l + jnp.sum(p, axis=-1, keepdims=True)
        acc = alpha * acc + jnp.dot(p.astype(BF16), v_ref[0, pl.ds(off, tk), :], preferred_element_type=F32)
        return m_new, l, acc

    init = (jnp.full((tq, 1), -jnp.inf, F32), jnp.zeros((tq, 1), F32), jnp.zeros((tq, V_DIM), F32))
    _, l, acc = lax.fori_loop(0, n_kv, body, init)
    o_ref[0] = (acc / l).astype(o_ref.dtype)


def _attention(q, kv, kr, *, tq, tk):
    bsz, nq, _ = q.shape
    nt = kv.shape[1]
    qk_w = QK_NOPE + 2 * QK_ROPE
    return pl.pallas_call(
        functools.partial(_attn_kernel, tk=tk, n_kv=nt // tk),
        grid=(bsz, MLA_HEADS, nq // tq),
        in_specs=[
            pl.BlockSpec((1, tq, qk_w), lambda b, h, i: (b, i, h)),
            pl.BlockSpec((1, nt, QK_NOPE), lambda b, h, i: (b, 0, 2 * h)),
            pl.BlockSpec((1, nt, 2 * QK_ROPE), lambda b, h, i: (b, 0, 0)),
            pl.BlockSpec((1, nt, V_DIM), lambda b, h, i: (b, 0, 2 * h + 1)),
        ],
        out_specs=pl.BlockSpec((1, tq, V_DIM), lambda b, h, i: (b, i, h)),
        out_shape=jax.ShapeDtypeStruct((bsz, nq, MLA_HEADS * V_DIM), BF16),
        scratch_shapes=[pltpu.VMEM((nt, qk_w), BF16)],
        compiler_params=_cparams(("arbitrary", "arbitrary", "arbitrary")),
        name="mla_attention",
    )(q, kv, kr, kv)


def _ffn_pre_kernel(x_ref, g_ref, mod_ref, rw_ref, rb_ref, h_ref, lg_ref, *, tm, rows_per_batch, n_ctx):
    i = pl.program_id(0)
    is_ctx = _is_ctx_rows(i, tm, rows_per_batch, n_ctx)
    h = _norm_rows(x_ref[...], g_ref[...])
    h = h * (1.0 + _mod_select(mod_ref, 4, is_ctx)) + _mod_select(mod_ref, 3, is_ctx)
    h_ref[...] = h
    lg_ref[...] = jnp.dot(h, rw_ref[...], precision=HIGHEST, preferred_element_type=F32) + rb_ref[...]


def _ffn_pre(x, g, mod, rw_pad, rb_pad, *, tm, rows_per_batch, n_ctx):
    rows, d = x.shape
    batch_of = lambda i: (i * tm) // rows_per_batch
    return pl.pallas_call(
        functools.partial(_ffn_pre_kernel, tm=tm, rows_per_batch=rows_per_batch, n_ctx=n_ctx),
        grid=(rows // tm,),
        in_specs=[
            pl.BlockSpec((tm, d), lambda i: (i, 0)),
            pl.BlockSpec((1, d), lambda i: (0, 0)),
            pl.BlockSpec((1, 2, 6, d), lambda i: (batch_of(i), 0, 0, 0)),
            pl.BlockSpec((d, LANE), lambda i: (0, 0)),
            pl.BlockSpec((1, LANE), lambda i: (0, 0)),
        ],
        out_specs=[pl.BlockSpec((tm, d), lambda i: (i, 0)), pl.BlockSpec((tm, LANE), lambda i: (i, 0))],
        out_shape=[jax.ShapeDtypeStruct((rows, d), F32), jax.ShapeDtypeStruct((rows, LANE), F32)],
        compiler_params=_cparams(("arbitrary",)),
        name="ffn_prenorm_router",
    )(x, g.reshape(1, d), mod, rw_pad, rb_pad)


def _row_copy(src_hbm, row, buf, slot, sem):
    return pltpu.make_async_copy(src_hbm.at[pl.ds(row, 1)], buf.at[pl.ds(slot, 1)], sem)


def _start_rows(src_hbm, ids_ref, n_rows, dst_row, buf, sem):
    def body(h, c):
        for q in range(DMA_QUEUES):
            r = h * DMA_QUEUES + q
            _row_copy(src_hbm, ids_ref[0, 0, r], buf, dst_row(r), sem).start(priority=q)
        return c
    lax.fori_loop(0, n_rows // DMA_QUEUES, body, 0, unroll=4)


def _wait_rows(src_hbm, n_rows, buf, sem):
    def body(r, c):
        _row_copy(src_hbm, 0, buf, r, sem).wait()
        return c
    lax.fori_loop(0, n_rows, body, 0, unroll=8)


def _gather_rows_kernel(nv_ref, idx_ref, idx_next_ref, src_hbm, o_ref, buf, sem, *, bm):
    i = pl.program_id(0)
    nv = nv_ref[0]
    cur = i % 2
    same_row = lambda r: r

    @pl.when(i == 0)
    def _():
        _start_rows(src_hbm, idx_ref, bm, same_row, buf.at[0], sem.at[0])

    @pl.when(i + 1 < nv)
    def _():
        _start_rows(src_hbm, idx_next_ref, bm, same_row, buf.at[1 - cur], sem.at[1 - cur])

    @pl.when(i < nv)
    def _():
        _wait_rows(src_hbm, bm, buf.at[cur], sem.at[cur])
        o_ref[...] = buf[cur].astype(o_ref.dtype)

    @pl.when(i >= nv)
    def _():
        o_ref[...] = jnp.zeros_like(o_ref)


def _gather_rows(src, idx, n_valid, *, bm, out_dtype):
    n_slots = idx.shape[0]
    d = src.shape[1]
    nb = n_slots // bm
    clamp = lambda i, nv: jnp.minimum(i, nv[0] - 1)
    idx3 = idx.reshape(nb, 1, bm)
    return pl.pallas_call(
        functools.partial(_gather_rows_kernel, bm=bm),
        grid_spec=pltpu.PrefetchScalarGridSpec(
            num_scalar_prefetch=1,
            grid=(nb,),
            in_specs=[
                pl.BlockSpec((1, 1, bm), lambda i, nv: (clamp(i, nv), 0, 0), memory_space=pltpu.SMEM),
                pl.BlockSpec((1, 1, bm), lambda i, nv: (clamp(i + 1, nv), 0, 0), memory_space=pltpu.SMEM),
                pl.BlockSpec(memory_space=pl.ANY),
            ],
            out_specs=pl.BlockSpec((bm, d), lambda i, nv: (i, 0)),
            scratch_shapes=[pltpu.VMEM((2, bm, d), src.dtype), pltpu.SemaphoreType.DMA((2,))],
        ),
        out_shape=jax.ShapeDtypeStruct((n_slots, d), out_dtype),
        compiler_params=_cparams(("arbitrary",)),
        name="moe_gather_rows",
    )(n_valid, idx3, idx3, src)


DEINT_CHUNK = 2 * LANE


def _deinterleave_perm():
    k_i = lax.broadcasted_iota(jnp.int32, (DEINT_CHUNK, DEINT_CHUNK), 0)
    n_i = lax.broadcasted_iota(jnp.int32, (DEINT_CHUNK, DEINT_CHUNK), 1)
    return (k_i == jnp.where(n_i < LANE, 2 * n_i, 2 * (n_i - LANE) + 1)).astype(BF16)


def _expert_changed(be_ref, i):
    return (i == 0) | (be_ref[i] != be_ref[jnp.maximum(i - 1, 0)])


def _moe_up_kernel(be_ref, nv_ref, x_ref, w_ref, p_ref, bg_ref, bl_ref, o_ref, wg_ref, wl_ref):
    i = pl.program_id(1)
    valid = i < nv_ref[0]

    @pl.when(valid & _expert_changed(be_ref, i))
    def _():
        perm = p_ref[...]
        for c in range(w_ref.shape[1] // DEINT_CHUNK):
            r = jnp.dot(w_ref[:, c * DEINT_CHUNK:(c + 1) * DEINT_CHUNK].astype(BF16), perm,
                        preferred_element_type=F32)
            wg_ref[:, c * LANE:(c + 1) * LANE] = r[:, :LANE].astype(BF16)
            wl_ref[:, c * LANE:(c + 1) * LANE] = r[:, LANE:].astype(BF16)

    @pl.when(valid)
    def _():
        x = x_ref[...]
        a_g = jnp.dot(x, wg_ref[...], preferred_element_type=F32) + bg_ref[...]
        a_l = jnp.dot(x, wl_ref[...], preferred_element_type=F32) + bl_ref[...]
        glu = jnp.minimum(a_g, SWIGLU_LIMIT)
        lin = jnp.clip(a_l, -SWIGLU_LIMIT, SWIGLU_LIMIT)
        o_ref[...] = (glu * jax.nn.sigmoid(SWIGLU_ALPHA * glu) * (lin + 1.0)).astype(o_ref.dtype)

    @pl.when(jnp.logical_not(valid))
    def _():
        o_ref[...] = jnp.zeros_like(o_ref)


def _moe_up(xs, w1_all, layer, b_glu, b_lin, block_expert, n_valid, *, tn):
    n_slots, d = xs.shape
    n_layers, n_e, _, f2 = w1_all.shape
    ff = f2 // 2
    nb = n_slots // MOE_ROWS
    blk = lambda i, nv: jnp.minimum(i, nv[0] - 1)
    bmap = lambda j, i, be, nv: (be[blk(i, nv)], 0, j)
    return pl.pallas_call(
        _moe_up_kernel,
        grid_spec=pltpu.PrefetchScalarGridSpec(
            num_scalar_prefetch=2,
            grid=(ff // tn, nb),
            in_specs=[
                pl.BlockSpec((MOE_ROWS, d), lambda j, i, be, nv: (blk(i, nv), 0)),
                pl.BlockSpec((None, d, 2 * tn), lambda j, i, be, nv: (layer * n_e + be[blk(i, nv)], 0, j)),
                pl.BlockSpec((DEINT_CHUNK, DEINT_CHUNK), lambda j, i, be, nv: (0, 0)),
                pl.BlockSpec((None, 1, tn), bmap),
                pl.BlockSpec((None, 1, tn), bmap),
            ],
            out_specs=pl.BlockSpec((MOE_ROWS, tn), lambda j, i, be, nv: (i, j)),
            scratch_shapes=[pltpu.VMEM((d, tn), BF16), pltpu.VMEM((d, tn), BF16)],
        ),
        out_shape=jax.ShapeDtypeStruct((n_slots, ff), BF16),
        compiler_params=_cparams(("arbitrary", "arbitrary")),
        name="moe_up",
    )(block_expert, n_valid, xs, w1_all.reshape(n_layers * n_e, d, f2), _deinterleave_perm(), b_glu, b_lin)


def _moe_down_kernel(be_ref, nv_ref, a_ref, w_ref, b_ref, o_ref, wb_ref):
    i = pl.program_id(1)
    valid = i < nv_ref[0]

    @pl.when(valid & _expert_changed(be_ref, i))
    def _():
        wb_ref[...] = w_ref[...].astype(BF16)

    @pl.when(valid)
    def _():
        o_ref[...] = jnp.dot(a_ref[...], wb_ref[...], preferred_element_type=F32) + b_ref[...]

    @pl.when(jnp.logical_not(valid))
    def _():
        o_ref[...] = jnp.zeros_like(o_ref)


def _moe_down(act, w2_all, layer, b2, block_expert, n_valid, *, tn):
    n_slots, ff = act.shape
    n_layers, n_e, _, d = w2_all.shape
    nb = n_slots // MOE_ROWS
    blk = lambda i, nv: jnp.minimum(i, nv[0] - 1)
    return pl.pallas_call(
        _moe_down_kernel,
        grid_spec=pltpu.PrefetchScalarGridSpec(
            num_scalar_prefetch=2,
            grid=(d // tn, nb),
            in_specs=[
                pl.BlockSpec((MOE_ROWS, ff), lambda j, i, be, nv: (blk(i, nv), 0)),
                pl.BlockSpec((None, ff, tn), lambda j, i, be, nv: (layer * n_e + be[blk(i, nv)], 0, j)),
                pl.BlockSpec((None, 1, tn), lambda j, i, be, nv: (be[blk(i, nv)], 0, j)),
            ],
            out_specs=pl.BlockSpec((MOE_ROWS, tn), lambda j, i, be, nv: (i, j)),
            scratch_shapes=[pltpu.VMEM((ff, tn), BF16)],
        ),
        out_shape=jax.ShapeDtypeStruct((n_slots, d), F32),
        compiler_params=_cparams(("arbitrary", "arbitrary")),
        name="moe_down",
    )(block_expert, n_valid, act, w2_all.reshape(n_layers * n_e, ff, d), b2)


def _moe_combine_kernel(pos_ref, pos_next_ref, y_hbm, gate_ref, x_ref, mod_ref, fg_ref, o_ref, buf, sem, *,
                        tm, rows_per_batch, n_ctx, final_norm):
    n_rows = TOP_K * tm
    i = pl.program_id(0)
    cur = i % 2
    choice_major = lambda r: (r % TOP_K) * tm + r // TOP_K

    @pl.when(i == 0)
    def _():
        _start_rows(y_hbm, pos_ref, n_rows, choice_major, buf.at[0], sem.at[0])

    @pl.when(i + 1 < pl.num_programs(0))
    def _():
        _start_rows(y_hbm, pos_next_ref, n_rows, choice_major, buf.at[1 - cur], sem.at[1 - cur])

    _wait_rows(y_hbm, n_rows, buf.at[cur], sem.at[cur])

    gates = gate_ref[...]
    f = gates[:, 0:1] * buf[cur, 0:tm, :]
    for kk in range(1, TOP_K):
        f = f + gates[:, kk:kk + 1] * buf[cur, kk * tm:(kk + 1) * tm, :]
    is_ctx = _is_ctx_rows(i, tm, rows_per_batch, n_ctx)
    out = x_ref[...] + _mod_select(mod_ref, 5, is_ctx) * f
    if final_norm:
        out = _norm_rows(out, fg_ref[...])
    o_ref[...] = out


def _moe_combine(y2, pos, gates, x, mod, final_g, *, tm, rows_per_batch, n_ctx, final_norm):
    rows, d = x.shape
    nblk = rows // tm
    batch_of = lambda i: (i * tm) // rows_per_batch
    pos3 = pos.reshape(nblk, 1, TOP_K * tm)
    return pl.pallas_call(
        functools.partial(_moe_combine_kernel, tm=tm, rows_per_batch=rows_per_batch, n_ctx=n_ctx,
                          final_norm=final_norm),
        grid=(nblk,),
        in_specs=[
            pl.BlockSpec((1, 1, TOP_K * tm), lambda i: (i, 0, 0), memory_space=pltpu.SMEM),
            pl.BlockSpec((1, 1, TOP_K * tm), lambda i: (jnp.minimum(i + 1, nblk - 1), 0, 0),
                         memory_space=pltpu.SMEM),
            pl.BlockSpec(memory_space=pl.ANY),
            pl.BlockSpec((tm, TOP_K), lambda i: (i, 0)),
            pl.BlockSpec((tm, d), lambda i: (i, 0)),
            pl.BlockSpec((1, 2, 6, d), lambda i: (batch_of(i), 0, 0, 0)),
            pl.BlockSpec((1, d), lambda i: (0, 0)),
        ],
        out_specs=pl.BlockSpec((tm, d), lambda i: (i, 0)),
        out_shape=jax.ShapeDtypeStruct((rows, d), F32),
        scratch_shapes=[pltpu.VMEM((2, TOP_K * tm, d), F32), pltpu.SemaphoreType.DMA((2,))],
        compiler_params=_cparams(("arbitrary",)),
        name="moe_combine",
    )(pos3, pos3, y2, gates, x, mod, final_g.reshape(1, d))


def _route(logits, n_experts):
    t_tok = logits.shape[0]
    top_val, top_idx = lax.top_k(logits, TOP_K)
    gates = jax.nn.softmax(top_val, axis=-1)
    flat_e = top_idx.reshape(-1).astype(jnp.int32)
    n_assign = t_tok * TOP_K
    experts = jnp.arange(n_experts, dtype=jnp.int32)
    counts = jnp.sum((flat_e[:, None] == experts[None, :]).astype(jnp.int32), axis=0)
    padded = (counts + MOE_ROWS - 1) // MOE_ROWS * MOE_ROWS
    incl = (experts[None, :] <= experts[:, None]).astype(jnp.int32)
    cum_pad = jnp.sum(incl * padded[None, :], axis=1)
    pad_start = cum_pad - padded
    start = jnp.sum(incl * counts[None, :], axis=1) - counts
    order = jnp.argsort(flat_e).astype(jnp.int32)
    rank = jnp.argsort(order).astype(jnp.int32)
    pos = pad_start[flat_e] + rank - start[flat_e]
    n_blocks = n_assign // MOE_ROWS + n_experts
    block_start = jnp.arange(n_blocks, dtype=jnp.int32) * MOE_ROWS
    block_expert = jnp.clip(jnp.sum((cum_pad[None, :] <= block_start[:, None]).astype(jnp.int32), axis=1),
                            0, n_experts - 1)
    slot_e = jnp.repeat(block_expert, MOE_ROWS)
    within = jnp.arange(n_blocks * MOE_ROWS, dtype=jnp.int32) - pad_start[slot_e]
    src = jnp.clip(start[slot_e] + within, 0, n_assign - 1)
    slot_tok = jnp.where(within < counts[slot_e], order[src] // TOP_K, 0)
    n_valid = cum_pad[-1:] // MOE_ROWS
    return gates, slot_tok, pos, block_expert, n_valid


def _moe_layer(x, norm_g, mod, router_w, router_b, w1_all, b1, w2_all, b2, final_g, *,
               layer, rows_per_batch, n_ctx, tm_rows, final_norm):
    rows, d = x.shape
    n_experts = router_w.shape[1]
    ff = w2_all.shape[2]
    rw_pad = jnp.pad(router_w, ((0, 0), (0, LANE - n_experts)))
    rb_pad = jnp.pad(router_b, (0, LANE - n_experts)).reshape(1, LANE)
    hf, logits = _ffn_pre(x, norm_g, mod, rw_pad, rb_pad, tm=tm_rows, rows_per_batch=rows_per_batch, n_ctx=n_ctx)
    gates, slot_tok, pos, block_expert, n_valid = _route(logits[:, :n_experts], n_experts)
    xs = _gather_rows(hf, slot_tok, n_valid, bm=MOE_ROWS, out_dtype=BF16)
    b_glu = b1[:, 0::2].reshape(n_experts, 1, ff)
    b_lin = b1[:, 1::2].reshape(n_experts, 1, ff)
    act = _moe_up(xs, w1_all, layer, b_glu, b_lin, block_expert, n_valid, tn=_pick(ff, (1024, 512, 256, 128)))
    y2 = _moe_down(act, w2_all, layer, b2.reshape(n_experts, 1, d), block_expert, n_valid,
                   tn=_pick(d, (2048, 1024, 512, 256, 128)))
    tm_c = _pick(rows_per_batch, (128,))
    return _moe_combine(y2, pos, gates, x, mod, final_g, tm=tm_c, rows_per_batch=rows_per_batch,
                        n_ctx=n_ctx, final_norm=final_norm)


def _modulation(c, c_ctx, ada_w, ada_b_i, layer):
    bsz, d = c.shape
    rows = 16
    s = jnp.concatenate([jax.nn.silu(c), jax.nn.silu(c_ctx)[None], jnp.zeros((rows - bsz - 1, d), F32)], axis=0)
    n = ada_w.shape[2]
    m = _fused_matmul(s, ada_w.reshape(-1, n), name="adaln_modulation", m_rows=rows, k=d, tm=rows,
                      tn=_pick(n, (1024, 512, 256, 128)), out_dtype=F32, rows_per_batch=rows,
                      w_row_block=layer, epilogue="bias", bias=ada_b_i)
    lat = m[:bsz].reshape(bsz, 6, d)
    ctx = jnp.broadcast_to(m[bsz].reshape(1, 6, d), (bsz, 6, d))
    return jnp.stack([ctx, lat], axis=1)


def _dwconv_silu(xbc, conv_w, conv_b, n_ctx):
    nt = xbc.shape[1]
    t = jnp.arange(nt)[None, :, None]
    prev = jnp.pad(xbc, ((0, 0), (1, 0), (0, 0)))[:, :nt]
    nxt = jnp.pad(xbc, ((0, 0), (0, 1), (0, 0)))[:, 1:]
    prev = jnp.where((t == 0) | (t == n_ctx), 0.0, prev)
    nxt = jnp.where((t == n_ctx - 1) | (t == nt - 1), 0.0, nxt)
    out = conv_b + prev * conv_w[0] + xbc * conv_w[1] + nxt * conv_w[2]
    return jax.nn.silu(out)


def _rope_tables(n_lat):
    rows = n_lat // GRID_W
    row = jnp.broadcast_to(jnp.arange(rows)[:, None], (rows, GRID_W)).reshape(-1)
    col = jnp.broadcast_to(jnp.arange(GRID_W)[None, :], (rows, GRID_W)).reshape(-1)
    freqs = ROPE_THETA ** (-jnp.arange(ROPE_PAIRS, dtype=F32) / ROPE_PAIRS)
    ang = jnp.stack([row[:, None] * freqs, col[:, None] * freqs], axis=1)
    cos, sin = jnp.cos(ang).astype(F32), jnp.sin(ang).astype(F32)
    c_tab = jnp.concatenate([cos[:, 0], cos[:, 0], cos[:, 1], cos[:, 1]], axis=-1)
    s_tab = jnp.concatenate([-sin[:, 0], sin[:, 0], -sin[:, 1], sin[:, 1]], axis=-1)
    return c_tab, s_tab


def _swap16(w):
    lane = jnp.arange(w.shape[-1])
    return jnp.take(w, jnp.where(lane % 32 < 16, lane + 16, lane - 16), axis=-1)


def kernel(x, c, ctx, c_ctx, mix_norm_g, ffn_norm_g, ada_w, ada_b, hyb_w_in, hyb_conv_w, hyb_conv_b, hyb_dt_bias, hyb_a_log, hyb_d_skip, hyb_ssd_norm_g, hyb_v_norm_g, hyb_w_s, hyb_b_s, hyb_w_out, mla_w_in, mla_q_norm_g, mla_kv_norm_g, mla_w_uq, mla_w_ukv, mla_w_o, router_w, router_b, exp_w1, exp_b1, exp_w2, exp_b2, final_norm_g):
    bsz, n_lat, d = x.shape
    n_ctx = ctx.shape[1]
    nt = n_ctx + n_lat
    rows = bsz * nt
    tm_s = _pick(nt, (768, 384, 256, 128))
    tm_l = _pick(n_lat, (1024, 512, 256, 128))

    stream = jnp.concatenate([ctx, x], axis=1).reshape(rows, d)

    mod0 = _modulation(c, c_ctx, ada_w, ada_b[0], 0)
    cuts = (SSD_WIDTH, SSD_WIDTH + XBC_WIDTH, SSD_WIDTH + XBC_WIDTH + 2 * SSD_HEADS)
    w_in = hyb_w_in[0]
    w_main = jnp.concatenate([w_in[:, :cuts[0]], w_in[:, cuts[2]:], w_in[:, cuts[0]:cuts[1]]], axis=1).astype(BF16)
    w_dt = jnp.pad(w_in[:, cuts[1]:cuts[2]], ((0, 0), (0, LANE - 2 * SSD_HEADS))).astype(BF16)
    pro = dict(prologue="norm_mod", g=mix_norm_g[0], mod=mod0, sh_idx=0, sc_idx=1,
               rows_per_batch=nt, n_ctx=n_ctx, m_rows=rows, k=d, tm=tm_s)
    proj = _fused_matmul(stream, w_main, name="hyb_in_proj", tn=_pick(w_main.shape[1], (1024, 512, 256, 128)), out_dtype=F32, **pro)
    dt_raw = _fused_matmul(stream, w_dt, name="hyb_dt_proj", tn=LANE, out_dtype=F32, **pro)

    xbc_off = SSD_WIDTH + 2 * GM_WIDTH
    xbc = _dwconv_silu(proj[:, xbc_off:].reshape(bsz, nt, XBC_WIDTH), hyb_conv_w[0], hyb_conv_b[0], n_ctx)
    dt_bias = jnp.pad(hyb_dt_bias[0].reshape(1, -1), ((0, 0), (0, LANE - 2 * SSD_HEADS)))
    a_log = jnp.pad(hyb_a_log[0].reshape(1, -1), ((0, 0), (0, LANE - 2 * SSD_HEADS)))
    dt3 = dt_raw.reshape(bsz, nt, LANE)
    n_ctx_chunks = n_ctx // SSD_CHUNK
    y_f = _ssd_scan(xbc, dt3, dt_bias, a_log, rev=False, n_ctx_chunks=n_ctx_chunks)
    y_b = _ssd_scan(xbc, dt3, dt_bias, a_log, rev=True, n_ctx_chunks=n_ctx_chunks)

    d_exp = jnp.repeat(hyb_d_skip[0], SSD_HEAD_DIM).reshape(1, SSD_WIDTH)
    y_cat = _mixer_merge(y_f.reshape(rows, SSD_WIDTH), y_b.reshape(rows, SSD_WIDTH),
                         xbc.reshape(rows, XBC_WIDTH), proj, d_exp,
                         hyb_ssd_norm_g[0].reshape(1, -1), hyb_v_norm_g[0].reshape(1, -1),
                         hyb_w_s[0].astype(BF16), hyb_b_s[0].T)
    stream = _fused_matmul(y_cat, hyb_w_out[0].astype(BF16), name="hyb_out_proj", m_rows=rows, k=SSD_WIDTH + GM_WIDTH, tm=tm_s,
                           tn=_pick(d, (1024, 512, 256, 128)), out_dtype=F32, rows_per_batch=nt, n_ctx=n_ctx,
                           epilogue="gated_resid", resid=stream, mod=mod0, gate_idx=2)
    stream = _moe_layer(stream, ffn_norm_g[0], mod0, router_w[0], router_b[0], exp_w1, exp_b1[0],
                        exp_w2, exp_b2[0], final_norm_g, layer=0, rows_per_batch=nt, n_ctx=n_ctx, tm_rows=tm_s,
                        final_norm=False)

    mod1 = _modulation(c, c_ctx, ada_w, ada_b[1], 1)
    w_in1 = mla_w_in[0]
    kpe_w = w_in1[:, Q_LORA + KV_LORA:]
    w_in1_r = jnp.concatenate([w_in1[:, Q_LORA:Q_LORA + KV_LORA], kpe_w, _swap16(kpe_w),
                               jnp.zeros((d, Q_LORA - KV_LORA - 2 * QK_ROPE), F32), w_in1[:, :Q_LORA]],
                              axis=1).astype(BF16)
    p1 = _fused_matmul(stream, w_in1_r, name="mla_in_proj", m_rows=rows, k=d, tm=tm_s, tn=w_in1_r.shape[1], out_dtype=F32,
                       prologue="norm_mod", g=mix_norm_g[1], mod=mod1, sh_idx=0, sc_idx=1,
                       rows_per_batch=nt, n_ctx=n_ctx)

    c_tab, s_tab = _rope_tables(n_lat)
    ck = jnp.concatenate([jnp.ones((n_ctx, QK_ROPE), F32), c_tab], axis=0)
    sk = jnp.concatenate([jnp.zeros((n_ctx, QK_ROPE), F32), s_tab], axis=0)
    p1_3 = p1.reshape(bsz, nt, -1)
    k_rot = p1_3[:, :, KV_LORA:KV_LORA + QK_ROPE] * ck + p1_3[:, :, KV_LORA + QK_ROPE:KV_LORA + 2 * QK_ROPE] * sk
    kr = jnp.concatenate([k_rot, k_rot], axis=-1).astype(BF16)

    kv = _fused_matmul(p1, mla_w_ukv[0].astype(BF16), name="mla_kv_up", m_rows=rows, k=KV_LORA, tm=tm_s,
                       tn=_pick(mla_w_ukv.shape[2], (2048, 1024, 512, 256, 128)), out_dtype=BF16,
                       rows_per_batch=nt, prologue="norm", g=mla_kv_norm_g[0], x_col_block=0)

    w_uq = mla_w_uq[0].reshape(Q_LORA, MLA_HEADS, QK_NOPE + QK_ROPE)
    w_q = jnp.concatenate([w_uq, _swap16(w_uq[:, :, QK_NOPE:])], axis=-1)
    w_q = w_q.reshape(Q_LORA, MLA_HEADS * (QK_NOPE + 2 * QK_ROPE)).astype(BF16)
    tm_q = _pick(n_ctx, (256, 128))
    lat_blocks = n_lat // tm_q
    q_rows = lambda i: (i // lat_blocks) * (nt // tm_q) + n_ctx // tm_q + i % lat_blocks
    q_tab = jnp.concatenate([jnp.ones((n_lat, QK_NOPE), F32), c_tab, s_tab], axis=-1) * (MLA_SCALE * math.log2(math.e))
    q = _fused_matmul(p1, w_q, name="mla_q_proj", m_rows=bsz * n_lat, k=Q_LORA, tm=tm_q, tn=w_q.shape[1],
                      out_dtype=BF16, rows_per_batch=n_lat, prologue="norm", g=mla_q_norm_g[0],
                      x_row_map=q_rows, x_col_block=1, epilogue="mul_table", table=q_tab,
                      table_row_map=lambda i: i % lat_blocks)

    o = _attention(q.reshape(bsz, n_lat, -1), kv.reshape(bsz, nt, -1), kr,
                   tq=_pick(n_lat, (1024, 512, 256, 128)), tk=_pick(nt, (2816, 768, 256, 128)))

    lat = stream.reshape(bsz, nt, d)[:, n_ctx:].reshape(bsz * n_lat, d)
    mod1_lat = mod1
    lat = _fused_matmul(o.reshape(bsz * n_lat, -1), mla_w_o[0].astype(BF16), name="mla_out_proj",
                        m_rows=bsz * n_lat,
                        k=MLA_HEADS * V_DIM, tm=tm_l, tn=_pick(d, (1024, 512, 256, 128)), out_dtype=F32,
                        rows_per_batch=n_lat, epilogue="gated_resid", resid=lat, mod=mod1_lat, gate_idx=2)
    out = _moe_layer(lat, ffn_norm_g[1], mod1_lat, router_w[1], router_b[1], exp_w1, exp_b1[1],
                     exp_w2, exp_b2[1], final_norm_g, layer=1, rows_per_batch=n_lat, n_ctx=0, tm_rows=tm_l,
                     final_norm=True)
    return out.reshape(bsz, n_lat, d)
```

```python
import functools
import math

import jax
import jax.numpy as jnp
from jax import lax
from jax.experimental import pallas as pl
from jax.experimental.pallas import tpu as pltpu

F32 = jnp.float32
BF16 = jnp.bfloat16
HIGHEST = lax.Precision.HIGHEST
EPS = 1e-6

GRID_W = 64
SSD_HEADS = 32
SSD_HEAD_DIM = 64
SSD_WIDTH = SSD_HEADS * SSD_HEAD_DIM
SSD_GROUPS = 4
SSD_HEADS_PER_GROUP = SSD_HEADS // SSD_GROUPS
SSD_STATE = 128
SSD_CHUNK = 128
SSD_GROUP_WIDTH = SSD_HEADS_PER_GROUP * SSD_HEAD_DIM
BC_WIDTH = SSD_GROUPS * SSD_STATE
XBC_WIDTH = SSD_WIDTH + 2 * BC_WIDTH
GM_GROUPS = 16
GM_GROUP_DIM = 128
GM_WIDTH = GM_GROUPS * GM_GROUP_DIM
GM_CHUNK = 128
MLA_HEADS = 16
Q_LORA = 768
KV_LORA = 512
QK_NOPE = 128
QK_ROPE = 64
V_DIM = 128
ROPE_PAIRS = QK_ROPE // 4
ROPE_THETA = 10000.0
MLA_SCALE = (QK_NOPE + QK_ROPE) ** -0.5
TOP_K = 4
SWIGLU_LIMIT = 7.0
SWIGLU_ALPHA = 1.702
MOE_ROWS = 256

LANE = 128
DMA_QUEUES = 2
VMEM_LIMIT = 56 * 1024 * 1024


def _pick(n, candidates):
    for c in candidates:
        if n % c == 0:
            return c
    raise ValueError(f"no tile for {n} in {candidates}")


def _cparams(sem):
    return pltpu.CompilerParams(dimension_semantics=sem, vmem_limit_bytes=VMEM_LIMIT)


def _is_ctx_rows(i, tm, rows_per_batch, n_ctx):
    row = (i * tm) % rows_per_batch + lax.broadcasted_iota(jnp.int32, (tm, 1), 0)
    return row < n_ctx


def _mod_select(mod_ref, idx, is_ctx):
    return jnp.where(is_ctx, mod_ref[0, 0, idx:idx + 1, :], mod_ref[0, 1, idx:idx + 1, :])


def _norm_rows(x, g):
    y = x * lax.rsqrt(jnp.mean(x * x, axis=-1, keepdims=True) + EPS)
    return y * g


def _fused_matmul_kernel(*refs, prologue, epilogue, tm, tn, rows_per_batch, n_ctx,
                         sh_idx, sc_idx, gate_idx, use_scratch):
    it = iter(refs)
    x_ref = next(it)
    g_ref = next(it) if prologue != "none" else None
    pmod_ref = next(it) if prologue == "norm_mod" else None
    w_ref = next(it)
    bias_ref = next(it) if epilogue == "bias" else None
    resid_ref = next(it) if epilogue == "gated_resid" else None
    emod_ref = next(it) if epilogue == "gated_resid" else None
    tab_ref = next(it) if epilogue == "mul_table" else None
    o_ref = next(it)
    xn_ref = next(it) if use_scratch else None

    i = pl.program_id(0)
    j = pl.program_id(1)

    if use_scratch:
        @pl.when(j == 0)
        def _():
            x = x_ref[...].astype(F32)
            if prologue != "none":
                x = _norm_rows(x, g_ref[...])
            if prologue == "norm_mod":
                is_ctx = _is_ctx_rows(i, tm, rows_per_batch, n_ctx)
                x = x * (1.0 + _mod_select(pmod_ref, sc_idx, is_ctx)) + _mod_select(pmod_ref, sh_idx, is_ctx)
            xn_ref[...] = x.astype(BF16)
        xn = xn_ref[...]
    else:
        xn = x_ref[...]

    acc = jnp.dot(xn, w_ref[...].astype(BF16), preferred_element_type=F32)
    if epilogue == "bias":
        acc = acc + bias_ref[...]
    elif epilogue == "gated_resid":
        is_ctx = _is_ctx_rows(i, tm, rows_per_batch, n_ctx)
        acc = resid_ref[...] + _mod_select(emod_ref, gate_idx, is_ctx) * acc
    elif epilogue == "mul_table":
        acc = acc * jnp.tile(tab_ref[...], (1, tn // tab_ref.shape[1]))
    o_ref[...] = acc.astype(o_ref.dtype)


def _fused_matmul(x, w, *, name, m_rows, k, tm, tn, out_dtype, rows_per_batch, n_ctx=0,
                  prologue="none", g=None, mod=None, sh_idx=0, sc_idx=0,
                  epilogue="none", bias=None, resid=None, gate_idx=0, table=None,
                  x_row_map=None, x_col_block=0, w_row_block=0, resid_row_map=None, table_row_map=None):
    n = w.shape[1]
    assert w.shape[0] % k == 0 and m_rows % tm == 0 and n % tn == 0 and rows_per_batch % tm == 0
    ident = lambda i: i
    x_row_map = x_row_map or ident
    resid_row_map = resid_row_map or ident
    table_row_map = table_row_map or ident
    batch_of = lambda i: (i * tm) // rows_per_batch
    use_scratch = prologue != "none" or x.dtype != BF16

    args = [x]
    in_specs = [pl.BlockSpec((tm, k), lambda i, j: (x_row_map(i), x_col_block))]
    if prologue != "none":
        args.append(g.reshape(1, k))
        in_specs.append(pl.BlockSpec((1, k), lambda i, j: (0, 0)))
    if prologue == "norm_mod":
        args.append(mod)
        in_specs.append(pl.BlockSpec((1, 2, 6, k), lambda i, j: (batch_of(i), 0, 0, 0)))
    args.append(w)
    in_specs.append(pl.BlockSpec((k, tn), lambda i, j: (w_row_block, j)))
    if epilogue == "bias":
        args.append(bias.reshape(1, n))
        in_specs.append(pl.BlockSpec((1, tn), lambda i, j: (0, j)))
    if epilogue == "gated_resid":
        args.append(resid)
        in_specs.append(pl.BlockSpec((tm, tn), lambda i, j: (resid_row_map(i), j)))
        args.append(mod)
        in_specs.append(pl.BlockSpec((1, 2, 6, tn), lambda i, j: (batch_of(i), 0, 0, j)))
    if epilogue == "mul_table":
        args.append(table)
        in_specs.append(pl.BlockSpec((tm, table.shape[1]), lambda i, j: (table_row_map(i), 0)))

    kern = functools.partial(
        _fused_matmul_kernel, prologue=prologue, epilogue=epilogue, tm=tm, tn=tn,
        rows_per_batch=rows_per_batch, n_ctx=n_ctx, sh_idx=sh_idx, sc_idx=sc_idx,
        gate_idx=gate_idx, use_scratch=use_scratch)
    return pl.pallas_call(
        kern,
        grid=(m_rows // tm, n // tn),
        in_specs=in_specs,
        out_specs=pl.BlockSpec((tm, tn), lambda i, j: (i, j)),
        out_shape=jax.ShapeDtypeStruct((m_rows, n), out_dtype),
        scratch_shapes=[pltpu.VMEM((tm, k), BF16)] if use_scratch else [],
        compiler_params=_cparams(("arbitrary", "arbitrary")),
        name=name,
    )(*args)


def _expand_heads(mat, col0):
    rows = mat.shape[0]
    return jnp.concatenate(
        [jnp.broadcast_to(mat[:, col0 + jj:col0 + jj + 1], (rows, SSD_HEAD_DIM))
         for jj in range(SSD_HEADS_PER_GROUP)], axis=1)


def _ssd_kernel(xs_ref, b_ref, c_ref, dtr_ref, dtb_ref, alog_ref, y_ref, h_ref, *, rev):
    L = SSD_CHUNK
    step = pl.program_id(1)

    @pl.when(step == 0)
    def _():
        h_ref[...] = jnp.zeros_like(h_ref)

    dcol = SSD_HEADS if rev else 0
    dt = jax.nn.softplus(dtr_ref[0] + dtb_ref[...])
    a = dt * (-jnp.exp(alog_ref[...]))
    r_i = lax.broadcasted_iota(jnp.int32, (L, L), 0)
    c_i = lax.broadcasted_iota(jnp.int32, (L, L), 1)
    tri = (r_i <= c_i) if rev else (r_i >= c_i)
    acs = jnp.dot(tri.astype(F32), a, precision=HIGHEST, preferred_element_type=F32)
    acs_t = acs.T
    edge = acs[0:1, :] if rev else acs[L - 1:L, :]
    decay_end = jnp.exp(edge - acs)
    exp_acs = jnp.exp(acs)
    chunk_decay = jnp.exp(edge)

    for g in range(SSD_GROUPS):
        col0 = dcol + g * SSD_HEADS_PER_GROUP
        bg = b_ref[0, :, g * SSD_STATE:(g + 1) * SSD_STATE]
        cg = c_ref[0, :, g * SSD_STATE:(g + 1) * SSD_STATE].astype(BF16)
        cb = lax.dot_general(cg, bg.astype(BF16), (((1,), (1,)), ((), ())), preferred_element_type=F32)
        xdt = xs_ref[0, :, g * SSD_GROUP_WIDTH:(g + 1) * SSD_GROUP_WIDTH] * _expand_heads(dt, col0)
        y_diag = []
        for jj in range(SSD_HEADS_PER_GROUP):
            col = col0 + jj
            seg = acs[:, col:col + 1] - acs_t[col:col + 1, :]
            lmat = jnp.exp(jnp.where(tri, seg, -jnp.inf))
            y_diag.append(jnp.dot((cb * lmat).astype(BF16),
                                  xdt[:, jj * SSD_HEAD_DIM:(jj + 1) * SSD_HEAD_DIM].astype(BF16),
                                  preferred_element_type=F32))
        y_diag = jnp.concatenate(y_diag, axis=1)
        h_g = h_ref[g]
        y_off = jnp.dot(cg, h_g.astype(BF16), preferred_element_type=F32) * _expand_heads(exp_acs, col0)
        y_ref[0, :, g * SSD_GROUP_WIDTH:(g + 1) * SSD_GROUP_WIDTH] = y_diag + y_off
        states = jnp.dot(bg.T.astype(BF16), (xdt * _expand_heads(decay_end, col0)).astype(BF16),
                         preferred_element_type=F32)
        h_ref[g] = h_g * _expand_heads(chunk_decay, col0) + states


def _ssd_scan(xbc, dt_raw, dt_bias, a_log, *, rev, n_ctx_chunks):
    bsz, nt, _ = xbc.shape
    nc = nt // SSD_CHUNK
    if rev:
        cmap = lambda s: jnp.where(s < n_ctx_chunks, n_ctx_chunks - 1 - s, nc - 1 + n_ctx_chunks - s)
    else:
        cmap = lambda s: s
    nb_x = SSD_WIDTH // BC_WIDTH
    return pl.pallas_call(
        functools.partial(_ssd_kernel, rev=rev),
        grid=(bsz, nc),
        in_specs=[
            pl.BlockSpec((1, SSD_CHUNK, SSD_WIDTH), lambda b, s: (b, cmap(s), 0)),
            pl.BlockSpec((1, SSD_CHUNK, BC_WIDTH), lambda b, s: (b, cmap(s), nb_x)),
            pl.BlockSpec((1, SSD_CHUNK, BC_WIDTH), lambda b, s: (b, cmap(s), nb_x + 1)),
            pl.BlockSpec((1, SSD_CHUNK, LANE), lambda b, s: (b, cmap(s), 0)),
            pl.BlockSpec((1, LANE), lambda b, s: (0, 0)),
            pl.BlockSpec((1, LANE), lambda b, s: (0, 0)),
        ],
        out_specs=pl.BlockSpec((1, SSD_CHUNK, SSD_WIDTH), lambda b, s: (b, cmap(s), 0)),
        out_shape=jax.ShapeDtypeStruct((bsz, nt, SSD_WIDTH), F32),
        scratch_shapes=[pltpu.VMEM((SSD_GROUPS, SSD_STATE, SSD_GROUP_WIDTH), F32)],
        compiler_params=_cparams(("arbitrary", "arbitrary")),
        name="ssd_scan_bwd" if rev else "ssd_scan_fwd",
    )(xbc, xbc, xbc, dt_raw, dt_bias, a_log)


def _mixer_merge_kernel(yf_ref, yb_ref, xs_ref, z_ref, u_ref, v_ref, dsk_ref, sg_ref, vg_ref,
                        ws_ref, bst_ref, o_ref):
    y = yf_ref[...] + yb_ref[...] + xs_ref[...] * dsk_ref[...]
    y_ssd = _norm_rows(y * jax.nn.silu(z_ref[...]), sg_ref[...])
    o_ref[:, :SSD_WIDTH] = y_ssd.astype(o_ref.dtype)
    u = jax.nn.gelu(u_ref[...])
    v = _norm_rows(jax.nn.gelu(v_ref[...]), vg_ref[...]).astype(BF16)
    for g in range(GM_GROUPS):
        sl = slice(g * GM_GROUP_DIM, (g + 1) * GM_GROUP_DIM)
        s = jnp.dot(ws_ref[g], v[:, sl], preferred_element_type=F32) + bst_ref[:, g:g + 1]
        o_ref[:, SSD_WIDTH + g * GM_GROUP_DIM:SSD_WIDTH + (g + 1) * GM_GROUP_DIM] = (u[:, sl] * s).astype(o_ref.dtype)


def _mixer_merge(yf, yb, xbc, proj, d_exp, ssd_g, v_g, w_s, b_s_t):
    rows = yf.shape[0]
    tm = GM_CHUNK
    row = lambda c: (lambda i: (i, c))
    const2 = lambda i: (0, 0)
    return pl.pallas_call(
        _mixer_merge_kernel,
        grid=(rows // tm,),
        in_specs=[
            pl.BlockSpec((tm, SSD_WIDTH), row(0)),
            pl.BlockSpec((tm, SSD_WIDTH), row(0)),
            pl.BlockSpec((tm, SSD_WIDTH), row(0)),
            pl.BlockSpec((tm, SSD_WIDTH), row(0)),
            pl.BlockSpec((tm, GM_WIDTH), row(1)),
            pl.BlockSpec((tm, GM_WIDTH), row(2)),
            pl.BlockSpec((1, SSD_WIDTH), const2),
            pl.BlockSpec((1, SSD_WIDTH), const2),
            pl.BlockSpec((1, GM_WIDTH), const2),
            pl.BlockSpec((GM_GROUPS, GM_CHUNK, GM_CHUNK), lambda i: (0, 0, 0)),
            pl.BlockSpec((GM_CHUNK, GM_GROUPS), const2),
        ],
        out_specs=pl.BlockSpec((tm, SSD_WIDTH + GM_WIDTH), lambda i: (i, 0)),
        out_shape=jax.ShapeDtypeStruct((rows, SSD_WIDTH + GM_WIDTH), BF16),
        compiler_params=_cparams(("arbitrary",)),
        name="mixer_merge",
    )(yf, yb, xbc, proj, proj, proj, d_exp, ssd_g, v_g, w_s, b_s_t)


def _attn_kernel(q_ref, kn_ref, kr_ref, v_ref, o_ref, kcat_ref, *, tk, n_kv):
    @pl.when(pl.program_id(2) == 0)
    def _():
        kcat_ref[:, :QK_NOPE] = kn_ref[0]
        kcat_ref[:, QK_NOPE:] = kr_ref[0]

    q = q_ref[0]
    tq = q.shape[0]

    def body(c, carry):
        m, l, acc = carry
        off = pl.multiple_of(c * tk, tk)
        s = lax.dot_general(q, kcat_ref[pl.ds(off, tk), :], (((1,), (1,)), ((), ())),
                            preferred_element_type=F32)
        m_new = jnp.maximum(m, jnp.max(s, axis=-1, keepdims=True))
        alpha = jnp.exp2(m - m_new)
        p = jnp.exp2(s - m_new)
        l = alpha * l + jnp.sum(p, axis=-1, keepdims=True)
        acc = alpha * acc + jnp.dot(p.astype(BF16), v_ref[0, pl.ds(off, tk), :], preferred_element_type=F32)
        return m_new, l, acc

    init = (jnp.full((tq, 1), -jnp.inf, F32), jnp.zeros((tq, 1), F32), jnp.zeros((tq, V_DIM), F32))
    _, l, acc = lax.fori_loop(0, n_kv, body, init)
    o_ref[0] = (acc / l).astype(o_ref.dtype)


def _attention(q, kv, kr, *, tq, tk):
    bsz, nq, _ = q.shape
    nt = kv.shape[1]
    qk_w = QK_NOPE + 2 * QK_ROPE
    return pl.pallas_call(
        functools.partial(_attn_kernel, tk=tk, n_kv=nt // tk),
        grid=(bsz, MLA_HEADS, nq // tq),
        in_specs=[
            pl.BlockSpec((1, tq, qk_w), lambda b, h, i: (b, i, h)),
            pl.BlockSpec((1, nt, QK_NOPE), lambda b, h, i: (b, 0, 2 * h)),
            pl.BlockSpec((1, nt, 2 * QK_ROPE), lambda b, h, i: (b, 0, 0)),
            pl.BlockSpec((1, nt, V_DIM), lambda b, h, i: (b, 0, 2 * h + 1)),
        ],
        out_specs=pl.BlockSpec((1, tq, V_DIM), lambda b, h, i: (b, i, h)),
        out_shape=jax.ShapeDtypeStruct((bsz, nq, MLA_HEADS * V_DIM), BF16),
        scratch_shapes=[pltpu.VMEM((nt, qk_w), BF16)],
        compiler_params=_cparams(("arbitrary", "arbitrary", "arbitrary")),
        name="mla_attention",
    )(q, kv, kr, kv)


def _ffn_pre_kernel(x_ref, g_ref, mod_ref, rw_ref, rb_ref, h_ref, lg_ref, *, tm, rows_per_batch, n_ctx):
    i = pl.program_id(0)
    is_ctx = _is_ctx_rows(i, tm, rows_per_batch, n_ctx)
    h = _norm_rows(x_ref[...], g_ref[...])
    h = h * (1.0 + _mod_select(mod_ref, 4, is_ctx)) + _mod_select(mod_ref, 3, is_ctx)
    h_ref[...] = h.reshape(h_ref.shape)
    lg_ref[...] = jnp.dot(h, rw_ref[...], precision=HIGHEST, preferred_element_type=F32) + rb_ref[...]


def _ffn_pre(x, g, mod, rw_pad, rb_pad, *, tm, rows_per_batch, n_ctx):
    rows, d = x.shape
    batch_of = lambda i: (i * tm) // rows_per_batch
    return pl.pallas_call(
        functools.partial(_ffn_pre_kernel, tm=tm, rows_per_batch=rows_per_batch, n_ctx=n_ctx),
        grid=(rows // tm,),
        in_specs=[
            pl.BlockSpec((tm, d), lambda i: (i, 0)),
            pl.BlockSpec((1, d), lambda i: (0, 0)),
            pl.BlockSpec((1, 2, 6, d), lambda i: (batch_of(i), 0, 0, 0)),
            pl.BlockSpec((d, LANE), lambda i: (0, 0)),
            pl.BlockSpec((1, LANE), lambda i: (0, 0)),
        ],
        out_specs=[pl.BlockSpec((tm, d // LANE, LANE), lambda i: (i, 0, 0)),
                   pl.BlockSpec((tm, LANE), lambda i: (i, 0))],
        out_shape=[jax.ShapeDtypeStruct((rows, d // LANE, LANE), F32), jax.ShapeDtypeStruct((rows, LANE), F32)],
        compiler_params=_cparams(("arbitrary",)),
        name="ffn_prenorm_router",
    )(x, g.reshape(1, d), mod, rw_pad, rb_pad)


def _row_copy(src_hbm, row, buf, slot, sem):
    return pltpu.make_async_copy(src_hbm.at[pl.ds(row, 1)], buf.at[pl.ds(slot, 1)], sem)


def _start_rows(src_hbm, ids_ref, n_rows, dst_row, buf, sem):
    def body(h, c):
        for q in range(DMA_QUEUES):
            r = h * DMA_QUEUES + q
            _row_copy(src_hbm, ids_ref[0, 0, r], buf, dst_row(r), sem).start(priority=q)
        return c
    lax.fori_loop(0, n_rows // DMA_QUEUES, body, 0, unroll=4)


def _wait_rows(src_hbm, n_rows, buf, sem):
    def body(r, c):
        _row_copy(src_hbm, 0, buf, r, sem).wait()
        return c
    lax.fori_loop(0, n_rows, body, 0, unroll=8)


def _gather_rows_kernel(nv_ref, idx_ref, idx_next_ref, src_hbm, o_ref, buf, sem, *, bm):
    i = pl.program_id(0)
    nv = nv_ref[0]
    cur = i % 2
    same_row = lambda r: r

    @pl.when(i == 0)
    def _():
        _start_rows(src_hbm, idx_ref, bm, same_row, buf.at[0], sem.at[0])

    @pl.when(i + 1 < nv)
    def _():
        _start_rows(src_hbm, idx_next_ref, bm, same_row, buf.at[1 - cur], sem.at[1 - cur])

    @pl.when(i < nv)
    def _():
        _wait_rows(src_hbm, bm, buf.at[cur], sem.at[cur])
        o_ref[...] = buf[cur].reshape(o_ref.shape).astype(o_ref.dtype)

    @pl.when(i >= nv)
    def _():
        o_ref[...] = jnp.zeros_like(o_ref)


def _gather_rows(src, idx, n_valid, *, bm, out_dtype):
    n_slots = idx.shape[0]
    row_shape = src.shape[1:]
    d = row_shape[0] * row_shape[1]
    nb = n_slots // bm
    clamp = lambda i, nv: jnp.minimum(i, nv[0] - 1)
    idx3 = idx.reshape(nb, 1, bm)
    return pl.pallas_call(
        functools.partial(_gather_rows_kernel, bm=bm),
        grid_spec=pltpu.PrefetchScalarGridSpec(
            num_scalar_prefetch=1,
            grid=(nb,),
            in_specs=[
                pl.BlockSpec((1, 1, bm), lambda i, nv: (clamp(i, nv), 0, 0), memory_space=pltpu.SMEM),
                pl.BlockSpec((1, 1, bm), lambda i, nv: (clamp(i + 1, nv), 0, 0), memory_space=pltpu.SMEM),
                pl.BlockSpec(memory_space=pl.ANY),
            ],
            out_specs=pl.BlockSpec((bm, d), lambda i, nv: (i, 0)),
            scratch_shapes=[pltpu.VMEM((2, bm) + row_shape, src.dtype), pltpu.SemaphoreType.DMA((2,))],
        ),
        out_shape=jax.ShapeDtypeStruct((n_slots, d), out_dtype),
        compiler_params=_cparams(("arbitrary",)),
        name="moe_gather_rows",
    )(n_valid, idx3, idx3, src)


DEINT_CHUNK = 2 * LANE


def _deinterleave_perm():
    k_i = lax.broadcasted_iota(jnp.int32, (DEINT_CHUNK, DEINT_CHUNK), 0)
    n_i = lax.broadcasted_iota(jnp.int32, (DEINT_CHUNK, DEINT_CHUNK), 1)
    return (k_i == jnp.where(n_i < LANE, 2 * n_i, 2 * (n_i - LANE) + 1)).astype(BF16)


def _expert_changed(be_ref, i):
    return (i == 0) | (be_ref[i] != be_ref[jnp.maximum(i - 1, 0)])


def _run_weights(w_hbm, wbuf, wsem, be_ref, run_ref, nxt_ref, i, layer_off):
    width = wbuf.shape[2]
    col = pl.multiple_of(pl.program_id(0) * width, width)
    slot = run_ref[i] % 2

    def copy(expert, s):
        return pltpu.make_async_copy(w_hbm.at[layer_off + expert, :, pl.ds(col, width)], wbuf.at[s], wsem.at[s])

    @pl.when(i == 0)
    def _():
        copy(be_ref[i], slot).start()

    copy(be_ref[i], slot).wait()

    @pl.when(nxt_ref[i] >= 0)
    def _():
        copy(nxt_ref[i], 1 - slot).start(priority=1)

    return slot


def _moe_up_kernel(be_ref, nv_ref, run_ref, nxt_ref, x_ref, w_hbm, p_ref, bg_ref, bl_ref, o_ref,
                   wbuf, wsem, wg_ref, wl_ref, *, layer_off):
    i = pl.program_id(1)
    valid = i < nv_ref[0]

    @pl.when(valid & _expert_changed(be_ref, i))
    def _():
        slot = _run_weights(w_hbm, wbuf, wsem, be_ref, run_ref, nxt_ref, i, layer_off)
        perm = p_ref[...]
        for c in range(wbuf.shape[2] // DEINT_CHUNK):
            r = jnp.dot(wbuf[slot, :, c * DEINT_CHUNK:(c + 1) * DEINT_CHUNK].astype(BF16), perm,
                        preferred_element_type=F32)
            wg_ref[:, c * LANE:(c + 1) * LANE] = r[:, :LANE].astype(BF16)
            wl_ref[:, c * LANE:(c + 1) * LANE] = r[:, LANE:].astype(BF16)

    @pl.when(valid)
    def _():
        x = x_ref[...]
        a_g = jnp.dot(x, wg_ref[...], preferred_element_type=F32) + bg_ref[...]
        a_l = jnp.dot(x, wl_ref[...], preferred_element_type=F32) + bl_ref[...]
        glu = jnp.minimum(a_g, SWIGLU_LIMIT)
        lin = jnp.clip(a_l, -SWIGLU_LIMIT, SWIGLU_LIMIT)
        o_ref[...] = (glu * jax.nn.sigmoid(SWIGLU_ALPHA * glu) * (lin + 1.0)).astype(o_ref.dtype)

    @pl.when(jnp.logical_not(valid))
    def _():
        o_ref[...] = jnp.zeros_like(o_ref)


def _moe_up(xs, w1_all, layer, b_glu, b_lin, routing, *, tn):
    n_slots, d = xs.shape
    n_layers, n_e, _, f2 = w1_all.shape
    ff = f2 // 2
    nb = n_slots // MOE_ROWS
    blk = lambda i, nv: jnp.minimum(i, nv[0] - 1)
    bmap = lambda j, i, be, nv, run, nxt: (be[blk(i, nv)], 0, j)
    return pl.pallas_call(
        functools.partial(_moe_up_kernel, layer_off=layer * n_e),
        grid_spec=pltpu.PrefetchScalarGridSpec(
            num_scalar_prefetch=4,
            grid=(ff // tn, nb),
            in_specs=[
                pl.BlockSpec((MOE_ROWS, d), lambda j, i, be, nv, run, nxt: (blk(i, nv), 0)),
                pl.BlockSpec(memory_space=pl.ANY),
                pl.BlockSpec((DEINT_CHUNK, DEINT_CHUNK), lambda j, i, be, nv, run, nxt: (0, 0)),
                pl.BlockSpec((None, 1, tn), bmap),
                pl.BlockSpec((None, 1, tn), bmap),
            ],
            out_specs=pl.BlockSpec((MOE_ROWS, tn), lambda j, i, be, nv, run, nxt: (i, j)),
            scratch_shapes=[pltpu.VMEM((2, d, 2 * tn), F32), pltpu.SemaphoreType.DMA((2,)),
                            pltpu.VMEM((d, tn), BF16), pltpu.VMEM((d, tn), BF16)],
        ),
        out_shape=jax.ShapeDtypeStruct((n_slots, ff), BF16),
        compiler_params=_cparams(("arbitrary", "arbitrary")),
        name="moe_up",
    )(*routing, xs, w1_all.reshape(n_layers * n_e, d, f2), _deinterleave_perm(), b_glu, b_lin)


def _moe_down_kernel(be_ref, nv_ref, run_ref, nxt_ref, a_ref, w_hbm, b_ref, o_ref, wbuf, wsem, wb_ref, *,
                     layer_off):
    i = pl.program_id(1)
    valid = i < nv_ref[0]

    @pl.when(valid & _expert_changed(be_ref, i))
    def _():
        slot = _run_weights(w_hbm, wbuf, wsem, be_ref, run_ref, nxt_ref, i, layer_off)
        wb_ref[...] = wbuf[slot].astype(BF16)

    @pl.when(valid)
    def _():
        y = jnp.dot(a_ref[...], wb_ref[...], preferred_element_type=F32) + b_ref[...]
        o_ref[...] = y.reshape(o_ref.shape)

    @pl.when(jnp.logical_not(valid))
    def _():
        o_ref[...] = jnp.zeros_like(o_ref)


def _moe_down(act, w2_all, layer, b2, routing):
    n_slots, ff = act.shape
    n_layers, n_e, _, d = w2_all.shape
    nb = n_slots // MOE_ROWS
    blk = lambda i, nv: jnp.minimum(i, nv[0] - 1)
    return pl.pallas_call(
        functools.partial(_moe_down_kernel, layer_off=layer * n_e),
        grid_spec=pltpu.PrefetchScalarGridSpec(
            num_scalar_prefetch=4,
            grid=(1, nb),
            in_specs=[
                pl.BlockSpec((MOE_ROWS, ff), lambda j, i, be, nv, run, nxt: (blk(i, nv), 0)),
                pl.BlockSpec(memory_space=pl.ANY),
                pl.BlockSpec((None, 1, d), lambda j, i, be, nv, run, nxt: (be[blk(i, nv)], 0, 0)),
            ],
            out_specs=pl.BlockSpec((MOE_ROWS, d // LANE, LANE), lambda j, i, be, nv, run, nxt: (i, 0, 0)),
            scratch_shapes=[pltpu.VMEM((2, ff, d), F32), pltpu.SemaphoreType.DMA((2,)), pltpu.VMEM((ff, d), BF16)],
        ),
        out_shape=jax.ShapeDtypeStruct((n_slots, d // LANE, LANE), F32),
        compiler_params=_cparams(("arbitrary", "arbitrary")),
        name="moe_down",
    )(*routing, act, w2_all.reshape(n_layers * n_e, ff, d), b2)


def _moe_combine_kernel(pos_ref, pos_next_ref, y_hbm, gate_ref, x_ref, mod_ref, fg_ref, o_ref, buf, sem, *,
                        tm, rows_per_batch, n_ctx, final_norm):
    n_rows = TOP_K * tm
    i = pl.program_id(0)
    cur = i % 2
    choice_major = lambda r: (r % TOP_K) * tm + r // TOP_K

    @pl.when(i == 0)
    def _():
        _start_rows(y_hbm, pos_ref, n_rows, choice_major, buf.at[0], sem.at[0])

    @pl.when(i + 1 < pl.num_programs(0))
    def _():
        _start_rows(y_hbm, pos_next_ref, n_rows, choice_major, buf.at[1 - cur], sem.at[1 - cur])

    _wait_rows(y_hbm, n_rows, buf.at[cur], sem.at[cur])

    gates = gate_ref[...]
    rows_of = lambda kk: buf[cur, kk * tm:(kk + 1) * tm].reshape(x_ref.shape)
    f = gates[:, 0:1] * rows_of(0)
    for kk in range(1, TOP_K):
        f = f + gates[:, kk:kk + 1] * rows_of(kk)
    is_ctx = _is_ctx_rows(i, tm, rows_per_batch, n_ctx)
    out = x_ref[...] + _mod_select(mod_ref, 5, is_ctx) * f
    if final_norm:
        out = _norm_rows(out, fg_ref[...])
    o_ref[...] = out


def _moe_combine(y2, pos, gates, x, mod, final_g, *, tm, rows_per_batch, n_ctx, final_norm):
    rows, d = x.shape
    nblk = rows // tm
    batch_of = lambda i: (i * tm) // rows_per_batch
    pos3 = pos.reshape(nblk, 1, TOP_K * tm)
    return pl.pallas_call(
        functools.partial(_moe_combine_kernel, tm=tm, rows_per_batch=rows_per_batch, n_ctx=n_ctx,
                          final_norm=final_norm),
        grid=(nblk,),
        in_specs=[
            pl.BlockSpec((1, 1, TOP_K * tm), lambda i: (i, 0, 0), memory_space=pltpu.SMEM),
            pl.BlockSpec((1, 1, TOP_K * tm), lambda i: (jnp.minimum(i + 1, nblk - 1), 0, 0),
                         memory_space=pltpu.SMEM),
            pl.BlockSpec(memory_space=pl.ANY),
            pl.BlockSpec((tm, TOP_K), lambda i: (i, 0)),
            pl.BlockSpec((tm, d), lambda i: (i, 0)),
            pl.BlockSpec((1, 2, 6, d), lambda i: (batch_of(i), 0, 0, 0)),
            pl.BlockSpec((1, d), lambda i: (0, 0)),
        ],
        out_specs=pl.BlockSpec((tm, d), lambda i: (i, 0)),
        out_shape=jax.ShapeDtypeStruct((rows, d), F32),
        scratch_shapes=[pltpu.VMEM((2, TOP_K * tm) + y2.shape[1:], F32), pltpu.SemaphoreType.DMA((2,))],
        compiler_params=_cparams(("arbitrary",)),
        name="moe_combine",
    )(pos3, pos3, y2, gates, x, mod, final_g.reshape(1, d))


def _route(logits, n_experts):
    t_tok = logits.shape[0]
    top_val, top_idx = lax.top_k(logits, TOP_K)
    gates = jax.nn.softmax(top_val, axis=-1)
    flat_e = top_idx.reshape(-1).astype(jnp.int32)
    n_assign = t_tok * TOP_K
    experts = jnp.arange(n_experts, dtype=jnp.int32)
    counts = jnp.sum((flat_e[:, None] == experts[None, :]).astype(jnp.int32), axis=0)
    padded = (counts + MOE_ROWS - 1) // MOE_ROWS * MOE_ROWS
    incl = (experts[None, :] <= experts[:, None]).astype(jnp.int32)
    cum_pad = jnp.sum(incl * padded[None, :], axis=1)
    pad_start = cum_pad - padded
    start = jnp.sum(incl * counts[None, :], axis=1) - counts
    order = jnp.argsort(flat_e).astype(jnp.int32)
    rank = jnp.argsort(order).astype(jnp.int32)
    pos = pad_start[flat_e] + rank - start[flat_e]
    n_blocks = n_assign // MOE_ROWS + n_experts
    block_start = jnp.arange(n_blocks, dtype=jnp.int32) * MOE_ROWS
    block_expert = jnp.clip(jnp.sum((cum_pad[None, :] <= block_start[:, None]).astype(jnp.int32), axis=1),
                            0, n_experts - 1)
    slot_e = jnp.repeat(block_expert, MOE_ROWS)
    within = jnp.arange(n_blocks * MOE_ROWS, dtype=jnp.int32) - pad_start[slot_e]
    src = jnp.clip(start[slot_e] + within, 0, n_assign - 1)
    slot_tok = jnp.where(within < counts[slot_e], order[src] // TOP_K, 0)
    n_valid = cum_pad[-1:] // MOE_ROWS
    blocks = jnp.arange(n_blocks, dtype=jnp.int32)
    is_valid = blocks < n_valid[0]
    starts = is_valid & ((blocks == 0) | (block_expert != jnp.roll(block_expert, 1)))
    run_id = jnp.sum((starts[None, :] & (blocks[None, :] <= blocks[:, None])).astype(jnp.int32), axis=1) - 1
    later_start = starts[None, :] & (blocks[None, :] > blocks[:, None])
    next_start = jnp.min(jnp.where(later_start, blocks[None, :], n_blocks), axis=1)
    next_expert = jnp.where(next_start < n_blocks, block_expert[jnp.minimum(next_start, n_blocks - 1)], -1)
    return gates, slot_tok, pos, (block_expert, n_valid, jnp.maximum(run_id, 0), next_expert)


def _moe_layer(x, norm_g, mod, router_w, router_b, w1_all, b1, w2_all, b2, final_g, *,
               layer, rows_per_batch, n_ctx, tm_rows, final_norm):
    rows, d = x.shape
    n_experts = router_w.shape[1]
    ff = w2_all.shape[2]
    rw_pad = jnp.pad(router_w, ((0, 0), (0, LANE - n_experts)))
    rb_pad = jnp.pad(router_b, (0, LANE - n_experts)).reshape(1, LANE)
    hf, logits = _ffn_pre(x, norm_g, mod, rw_pad, rb_pad, tm=tm_rows, rows_per_batch=rows_per_batch, n_ctx=n_ctx)
    gates, slot_tok, pos, routing = _route(logits[:, :n_experts], n_experts)
    xs = _gather_rows(hf, slot_tok, routing[1], bm=MOE_ROWS, out_dtype=BF16)
    b_glu = b1[:, 0::2].reshape(n_experts, 1, ff)
    b_lin = b1[:, 1::2].reshape(n_experts, 1, ff)
    act = _moe_up(xs, w1_all, layer, b_glu, b_lin, routing, tn=_pick(ff, (1024, 512, 256, 128)))
    y2 = _moe_down(act, w2_all, layer, b2.reshape(n_experts, 1, d), routing)
    tm_c = _pick(rows_per_batch, (128,))
    return _moe_combine(y2, pos, gates, x, mod, final_g, tm=tm_c, rows_per_batch=rows_per_batch,
                        n_ctx=n_ctx, final_norm=final_norm)


def _modulation(c, c_ctx, ada_w, ada_b_i, layer):
    bsz, d = c.shape
    rows = 16
    s = jnp.concatenate([jax.nn.silu(c), jax.nn.silu(c_ctx)[None], jnp.zeros((rows - bsz - 1, d), F32)], axis=0)
    n = ada_w.shape[2]
    m = _fused_matmul(s, ada_w.reshape(-1, n), name="adaln_modulation", m_rows=rows, k=d, tm=rows,
                      tn=_pick(n, (1024, 512, 256, 128)), out_dtype=F32, rows_per_batch=rows,
                      w_row_block=layer, epilogue="bias", bias=ada_b_i)
    lat = m[:bsz].reshape(bsz, 6, d)
    ctx = jnp.broadcast_to(m[bsz].reshape(1, 6, d), (bsz, 6, d))
    return jnp.stack([ctx, lat], axis=1)


def _dwconv_silu(xbc, conv_w, conv_b, n_ctx):
    nt = xbc.shape[1]
    t = jnp.arange(nt)[None, :, None]
    prev = jnp.pad(xbc, ((0, 0), (1, 0), (0, 0)))[:, :nt]
    nxt = jnp.pad(xbc, ((0, 0), (0, 1), (0, 0)))[:, 1:]
    prev = jnp.where((t == 0) | (t == n_ctx), 0.0, prev)
    nxt = jnp.where((t == n_ctx - 1) | (t == nt - 1), 0.0, nxt)
    out = conv_b + prev * conv_w[0] + xbc * conv_w[1] + nxt * conv_w[2]
    return jax.nn.silu(out)


def _rope_tables(n_lat):
    rows = n_lat // GRID_W
    row = jnp.broadcast_to(jnp.arange(rows)[:, None], (rows, GRID_W)).reshape(-1)
    col = jnp.broadcast_to(jnp.arange(GRID_W)[None, :], (rows, GRID_W)).reshape(-1)
    freqs = ROPE_THETA ** (-jnp.arange(ROPE_PAIRS, dtype=F32) / ROPE_PAIRS)
    ang = jnp.stack([row[:, None] * freqs, col[:, None] * freqs], axis=1)
    cos, sin = jnp.cos(ang).astype(F32), jnp.sin(ang).astype(F32)
    c_tab = jnp.concatenate([cos[:, 0], cos[:, 0], cos[:, 1], cos[:, 1]], axis=-1)
    s_tab = jnp.concatenate([-sin[:, 0], sin[:, 0], -sin[:, 1], sin[:, 1]], axis=-1)
    return c_tab, s_tab


def _swap16(w):
    lane = jnp.arange(w.shape[-1])
    return jnp.take(w, jnp.where(lane % 32 < 16, lane + 16, lane - 16), axis=-1)


def kernel(x, c, ctx, c_ctx, mix_norm_g, ffn_norm_g, ada_w, ada_b, hyb_w_in, hyb_conv_w, hyb_conv_b, hyb_dt_bias, hyb_a_log, hyb_d_skip, hyb_ssd_norm_g, hyb_v_norm_g, hyb_w_s, hyb_b_s, hyb_w_out, mla_w_in, mla_q_norm_g, mla_kv_norm_g, mla_w_uq, mla_w_ukv, mla_w_o, router_w, router_b, exp_w1, exp_b1, exp_w2, exp_b2, final_norm_g):
    bsz, n_lat, d = x.shape
    n_ctx = ctx.shape[1]
    nt = n_ctx + n_lat
    rows = bsz * nt
    tm_s = _pick(nt, (768, 384, 256, 128))
    tm_l = _pick(n_lat, (1024, 512, 256, 128))

    stream = jnp.concatenate([ctx, x], axis=1).reshape(rows, d)

    mod0 = _modulation(c, c_ctx, ada_w, ada_b[0], 0)
    cuts = (SSD_WIDTH, SSD_WIDTH + XBC_WIDTH, SSD_WIDTH + XBC_WIDTH + 2 * SSD_HEADS)
    w_in = hyb_w_in[0]
    w_main = jnp.concatenate([w_in[:, :cuts[0]], w_in[:, cuts[2]:], w_in[:, cuts[0]:cuts[1]]], axis=1).astype(BF16)
    w_dt = jnp.pad(w_in[:, cuts[1]:cuts[2]], ((0, 0), (0, LANE - 2 * SSD_HEADS))).astype(BF16)
    pro = dict(prologue="norm_mod", g=mix_norm_g[0], mod=mod0, sh_idx=0, sc_idx=1,
               rows_per_batch=nt, n_ctx=n_ctx, m_rows=rows, k=d, tm=tm_s)
    proj = _fused_matmul(stream, w_main, name="hyb_in_proj", tn=_pick(w_main.shape[1], (1024, 512, 256, 128)), out_dtype=F32, **pro)
    dt_raw = _fused_matmul(stream, w_dt, name="hyb_dt_proj", tn=LANE, out_dtype=F32, **pro)

    xbc_off = SSD_WIDTH + 2 * GM_WIDTH
    xbc = _dwconv_silu(proj[:, xbc_off:].reshape(bsz, nt, XBC_WIDTH), hyb_conv_w[0], hyb_conv_b[0], n_ctx)
    dt_bias = jnp.pad(hyb_dt_bias[0].reshape(1, -1), ((0, 0), (0, LANE - 2 * SSD_HEADS)))
    a_log = jnp.pad(hyb_a_log[0].reshape(1, -1), ((0, 0), (0, LANE - 2 * SSD_HEADS)))
    dt3 = dt_raw.reshape(bsz, nt, LANE)
    n_ctx_chunks = n_ctx // SSD_CHUNK
    y_f = _ssd_scan(xbc, dt3, dt_bias, a_log, rev=False, n_ctx_chunks=n_ctx_chunks)
    y_b = _ssd_scan(xbc, dt3, dt_bias, a_log, rev=True, n_ctx_chunks=n_ctx_chunks)

    d_exp = jnp.repeat(hyb_d_skip[0], SSD_HEAD_DIM).reshape(1, SSD_WIDTH)
    y_cat = _mixer_merge(y_f.reshape(rows, SSD_WIDTH), y_b.reshape(rows, SSD_WIDTH),
                         xbc.reshape(rows, XBC_WIDTH), proj, d_exp,
                         hyb_ssd_norm_g[0].reshape(1, -1), hyb_v_norm_g[0].reshape(1, -1),
                         hyb_w_s[0].astype(BF16), hyb_b_s[0].T)
    stream = _fused_matmul(y_cat, hyb_w_out[0].astype(BF16), name="hyb_out_proj", m_rows=rows, k=SSD_WIDTH + GM_WIDTH, tm=tm_s,
                           tn=_pick(d, (1024, 512, 256, 128)), out_dtype=F32, rows_per_batch=nt, n_ctx=n_ctx,
                           epilogue="gated_resid", resid=stream, mod=mod0, gate_idx=2)
    stream = _moe_layer(stream, ffn_norm_g[0], mod0, router_w[0], router_b[0], exp_w1, exp_b1[0],
                        exp_w2, exp_b2[0], final_norm_g, layer=0, rows_per_batch=nt, n_ctx=n_ctx, tm_rows=tm_s,
                        final_norm=False)

    mod1 = _modulation(c, c_ctx, ada_w, ada_b[1], 1)
    w_in1 = mla_w_in[0]
    kpe_w = w_in1[:, Q_LORA + KV_LORA:]
    w_in1_r = jnp.concatenate([w_in1[:, Q_LORA:Q_LORA + KV_LORA], kpe_w, _swap16(kpe_w),
                               jnp.zeros((d, Q_LORA - KV_LORA - 2 * QK_ROPE), F32), w_in1[:, :Q_LORA]],
                              axis=1).astype(BF16)
    p1 = _fused_matmul(stream, w_in1_r, name="mla_in_proj", m_rows=rows, k=d, tm=tm_s, tn=w_in1_r.shape[1], out_dtype=F32,
                       prologue="norm_mod", g=mix_norm_g[1], mod=mod1, sh_idx=0, sc_idx=1,
                       rows_per_batch=nt, n_ctx=n_ctx)

    c_tab, s_tab = _rope_tables(n_lat)
    ck = jnp.concatenate([jnp.ones((n_ctx, QK_ROPE), F32), c_tab], axis=0)
    sk = jnp.concatenate([jnp.zeros((n_ctx, QK_ROPE), F32), s_tab], axis=0)
    p1_3 = p1.reshape(bsz, nt, -1)
    k_rot = p1_3[:, :, KV_LORA:KV_LORA + QK_ROPE] * ck + p1_3[:, :, KV_LORA + QK_ROPE:KV_LORA + 2 * QK_ROPE] * sk
    kr = jnp.concatenate([k_rot, k_rot], axis=-1).astype(BF16)

    kv = _fused_matmul(p1, mla_w_ukv[0].astype(BF16), name="mla_kv_up", m_rows=rows, k=KV_LORA, tm=tm_s,
                       tn=_pick(mla_w_ukv.shape[2], (2048, 1024, 512, 256, 128)), out_dtype=BF16,
                       rows_per_batch=nt, prologue="norm", g=mla_kv_norm_g[0], x_col_block=0)

    w_uq = mla_w_uq[0].reshape(Q_LORA, MLA_HEADS, QK_NOPE + QK_ROPE)
    w_q = jnp.concatenate([w_uq, _swap16(w_uq[:, :, QK_NOPE:])], axis=-1)
    w_q = w_q.reshape(Q_LORA, MLA_HEADS * (QK_NOPE + 2 * QK_ROPE)).astype(BF16)
    tm_q = _pick(n_ctx, (256, 128))
    lat_blocks = n_lat // tm_q
    q_rows = lambda i: (i // lat_blocks) * (nt // tm_q) + n_ctx // tm_q + i % lat_blocks
    q_tab = jnp.concatenate([jnp.ones((n_lat, QK_NOPE), F32), c_tab, s_tab], axis=-1) * (MLA_SCALE * math.log2(math.e))
    q = _fused_matmul(p1, w_q, name="mla_q_proj", m_rows=bsz * n_lat, k=Q_LORA, tm=tm_q, tn=w_q.shape[1],
                      out_dtype=BF16, rows_per_batch=n_lat, prologue="norm", g=mla_q_norm_g[0],
                      x_row_map=q_rows, x_col_block=1, epilogue="mul_table", table=q_tab,
                      table_row_map=lambda i: i % lat_blocks)

    o = _attention(q.reshape(bsz, n_lat, -1), kv.reshape(bsz, nt, -1), kr,
                   tq=_pick(n_lat, (1024, 512, 256, 128)), tk=_pick(nt, (2816, 768, 256, 128)))

    lat = stream.reshape(bsz, nt, d)[:, n_ctx:].reshape(bsz * n_lat, d)
    mod1_lat = mod1
    lat = _fused_matmul(o.reshape(bsz * n_lat, -1), mla_w_o[0].astype(BF16), name="mla_out_proj",
                        m_rows=bsz * n_lat,
                        k=MLA_HEADS * V_DIM, tm=tm_l, tn=_pick(d, (1024, 512, 256, 128)), out_dtype=F32,
                        rows_per_batch=n_lat, epilogue="gated_resid", resid=lat, mod=mod1_lat, gate_idx=2)
    out = _moe_layer(lat, ffn_norm_g[1], mod1_lat, router_w[1], router_b[1], exp_w1, exp_b1[1],
                     exp_w2, exp_b2[1], final_norm_g, layer=1, rows_per_batch=n_lat, n_ctx=0, tm_rows=tm_l,
                     final_norm=True)
    return out.reshape(bsz, n_lat, d)
```

```python
import functools
import math

import jax
import jax.numpy as jnp
from jax import lax
from jax.experimental import pallas as pl
from jax.experimental.pallas import tpu as pltpu

F32 = jnp.float32
BF16 = jnp.bfloat16
HIGHEST = lax.Precision.HIGHEST
EPS = 1e-6

GRID_W = 64
SSD_HEADS = 32
SSD_HEAD_DIM = 64
SSD_WIDTH = SSD_HEADS * SSD_HEAD_DIM
SSD_GROUPS = 4
SSD_HEADS_PER_GROUP = SSD_HEADS // SSD_GROUPS
SSD_STATE = 128
SSD_CHUNK = 128
SSD_GROUP_WIDTH = SSD_HEADS_PER_GROUP * SSD_HEAD_DIM
BC_WIDTH = SSD_GROUPS * SSD_STATE
XBC_WIDTH = SSD_WIDTH + 2 * BC_WIDTH
GM_GROUPS = 16
GM_GROUP_DIM = 128
GM_WIDTH = GM_GROUPS * GM_GROUP_DIM
GM_CHUNK = 128
MLA_HEADS = 16
Q_LORA = 768
KV_LORA = 512
QK_NOPE = 128
QK_ROPE = 64
V_DIM = 128
ROPE_PAIRS = QK_ROPE // 4
ROPE_THETA = 10000.0
MLA_SCALE = (QK_NOPE + QK_ROPE) ** -0.5
TOP_K = 4
SWIGLU_LIMIT = 7.0
SWIGLU_ALPHA = 1.702
MOE_ROWS = 256

LANE = 128
DMA_QUEUES = 2
VMEM_LIMIT = 56 * 1024 * 1024


def _pick(n, candidates):
    for c in candidates:
        if n % c == 0:
            return c
    raise ValueError(f"no tile for {n} in {candidates}")


def _cparams(sem):
    return pltpu.CompilerParams(dimension_semantics=sem, vmem_limit_bytes=VMEM_LIMIT)


def _is_ctx_rows(i, tm, rows_per_batch, n_ctx):
    row = (i * tm) % rows_per_batch + lax.broadcasted_iota(jnp.int32, (tm, 1), 0)
    return row < n_ctx


def _mod_select(mod_ref, idx, is_ctx):
    return jnp.where(is_ctx, mod_ref[0, 0, idx:idx + 1, :], mod_ref[0, 1, idx:idx + 1, :])


def _norm_rows(x, g):
    y = x * lax.rsqrt(jnp.mean(x * x, axis=-1, keepdims=True) + EPS)
    return y * g


def _fused_matmul_kernel(*refs, prologue, epilogue, tm, tn, rows_per_batch, n_ctx,
                         sh_idx, sc_idx, gate_idx, use_scratch):
    it = iter(refs)
    x_ref = next(it)
    g_ref = next(it) if prologue != "none" else None
    pmod_ref = next(it) if prologue == "norm_mod" else None
    w_ref = next(it)
    bias_ref = next(it) if epilogue == "bias" else None
    resid_ref = next(it) if epilogue == "gated_resid" else None
    emod_ref = next(it) if epilogue == "gated_resid" else None
    tab_ref = next(it) if epilogue == "mul_table" else None
    o_ref = next(it)
    xn_ref = next(it) if use_scratch else None

    i = pl.program_id(0)
    j = pl.program_id(1)

    if use_scratch:
        @pl.when(j == 0)
        def _():
            x = x_ref[...].astype(F32)
            if prologue != "none":
                x = _norm_rows(x, g_ref[...])
            if prologue == "norm_mod":
                is_ctx = _is_ctx_rows(i, tm, rows_per_batch, n_ctx)
                x = x * (1.0 + _mod_select(pmod_ref, sc_idx, is_ctx)) + _mod_select(pmod_ref, sh_idx, is_ctx)
            xn_ref[...] = x.astype(BF16)
        xn = xn_ref[...]
    else:
        xn = x_ref[...]

    acc = jnp.dot(xn, w_ref[...].astype(BF16), preferred_element_type=F32)
    if epilogue == "bias":
        acc = acc + bias_ref[...]
    elif epilogue == "gated_resid":
        is_ctx = _is_ctx_rows(i, tm, rows_per_batch, n_ctx)
        acc = resid_ref[...] + _mod_select(emod_ref, gate_idx, is_ctx) * acc
    elif epilogue == "mul_table":
        acc = acc * jnp.tile(tab_ref[...], (1, tn // tab_ref.shape[1]))
    o_ref[...] = acc.astype(o_ref.dtype)


def _fused_matmul(x, w, *, name, m_rows, k, tm, tn, out_dtype, rows_per_batch, n_ctx=0,
                  prologue="none", g=None, mod=None, sh_idx=0, sc_idx=0,
                  epilogue="none", bias=None, resid=None, gate_idx=0, table=None,
                  x_row_map=None, x_col_block=0, w_row_block=0, resid_row_map=None, table_row_map=None):
    n = w.shape[1]
    assert w.shape[0] % k == 0 and m_rows % tm == 0 and n % tn == 0 and rows_per_batch % tm == 0
    ident = lambda i: i
    x_row_map = x_row_map or ident
    resid_row_map = resid_row_map or ident
    table_row_map = table_row_map or ident
    batch_of = lambda i: (i * tm) // rows_per_batch
    use_scratch = prologue != "none" or x.dtype != BF16

    args = [x]
    in_specs = [pl.BlockSpec((tm, k), lambda i, j: (x_row_map(i), x_col_block))]
    if prologue != "none":
        args.append(g.reshape(1, k))
        in_specs.append(pl.BlockSpec((1, k), lambda i, j: (0, 0)))
    if prologue == "norm_mod":
        args.append(mod)
        in_specs.append(pl.BlockSpec((1, 2, 6, k), lambda i, j: (batch_of(i), 0, 0, 0)))
    args.append(w)
    in_specs.append(pl.BlockSpec((k, tn), lambda i, j: (w_row_block, j)))
    if epilogue == "bias":
        args.append(bias.reshape(1, n))
        in_specs.append(pl.BlockSpec((1, tn), lambda i, j: (0, j)))
    if epilogue == "gated_resid":
        args.append(resid)
        in_specs.append(pl.BlockSpec((tm, tn), lambda i, j: (resid_row_map(i), j)))
        args.append(mod)
        in_specs.append(pl.BlockSpec((1, 2, 6, tn), lambda i, j: (batch_of(i), 0, 0, j)))
    if epilogue == "mul_table":
        args.append(table)
        in_specs.append(pl.BlockSpec((tm, table.shape[1]), lambda i, j: (table_row_map(i), 0)))

    kern = functools.partial(
        _fused_matmul_kernel, prologue=prologue, epilogue=epilogue, tm=tm, tn=tn,
        rows_per_batch=rows_per_batch, n_ctx=n_ctx, sh_idx=sh_idx, sc_idx=sc_idx,
        gate_idx=gate_idx, use_scratch=use_scratch)
    return pl.pallas_call(
        kern,
        grid=(m_rows // tm, n // tn),
        in_specs=in_specs,
        out_specs=pl.BlockSpec((tm, tn), lambda i, j: (i, j)),
        out_shape=jax.ShapeDtypeStruct((m_rows, n), out_dtype),
        scratch_shapes=[pltpu.VMEM((tm, k), BF16)] if use_scratch else [],
        compiler_params=_cparams(("arbitrary", "arbitrary")),
        name=name,
    )(*args)


def _expand_heads(mat, col0):
    rows = mat.shape[0]
    return jnp.concatenate(
        [jnp.broadcast_to(mat[:, col0 + jj:col0 + jj + 1], (rows, SSD_HEAD_DIM))
         for jj in range(SSD_HEADS_PER_GROUP)], axis=1)


def _ssd_kernel(xs_ref, b_ref, c_ref, dtr_ref, dtb_ref, alog_ref, y_ref, h_ref, *, rev):
    L = SSD_CHUNK
    step = pl.program_id(1)

    @pl.when(step == 0)
    def _():
        h_ref[...] = jnp.zeros_like(h_ref)

    dcol = SSD_HEADS if rev else 0
    dt = jax.nn.softplus(dtr_ref[0] + dtb_ref[...])
    a = dt * (-jnp.exp(alog_ref[...]))
    r_i = lax.broadcasted_iota(jnp.int32, (L, L), 0)
    c_i = lax.broadcasted_iota(jnp.int32, (L, L), 1)
    tri = (r_i <= c_i) if rev else (r_i >= c_i)
    acs = jnp.dot(tri.astype(F32), a, precision=HIGHEST, preferred_element_type=F32)
    acs_t = acs.T
    edge = acs[0:1, :] if rev else acs[L - 1:L, :]
    decay_end = jnp.exp(edge - acs)
    exp_acs = jnp.exp(acs)
    chunk_decay = jnp.exp(edge)

    for g in range(SSD_GROUPS):
        col0 = dcol + g * SSD_HEADS_PER_GROUP
        bg = b_ref[0, :, g * SSD_STATE:(g + 1) * SSD_STATE]
        cg = c_ref[0, :, g * SSD_STATE:(g + 1) * SSD_STATE].astype(BF16)
        cb = lax.dot_general(cg, bg.astype(BF16), (((1,), (1,)), ((), ())), preferred_element_type=F32)
        xdt = xs_ref[0, :, g * SSD_GROUP_WIDTH:(g + 1) * SSD_GROUP_WIDTH] * _expand_heads(dt, col0)
        y_diag = []
        for jj in range(SSD_HEADS_PER_GROUP):
            col = col0 + jj
            seg = acs[:, col:col + 1] - acs_t[col:col + 1, :]
            lmat = jnp.exp(jnp.where(tri, seg, -jnp.inf))
            y_diag.append(jnp.dot((cb * lmat).astype(BF16),
                                  xdt[:, jj * SSD_HEAD_DIM:(jj + 1) * SSD_HEAD_DIM].astype(BF16),
                                  preferred_element_type=F32))
        y_diag = jnp.concatenate(y_diag, axis=1)
        h_g = h_ref[g]
        y_off = jnp.dot(cg, h_g.astype(BF16), preferred_element_type=F32) * _expand_heads(exp_acs, col0)
        y_ref[0, :, g * SSD_GROUP_WIDTH:(g + 1) * SSD_GROUP_WIDTH] = y_diag + y_off
        states = jnp.dot(bg.T.astype(BF16), (xdt * _expand_heads(decay_end, col0)).astype(BF16),
                         preferred_element_type=F32)
        h_ref[g] = h_g * _expand_heads(chunk_decay, col0) + states


def _ssd_scan(xbc, dt_raw, dt_bias, a_log, *, rev, n_ctx_chunks):
    bsz, nt, _ = xbc.shape
    nc = nt // SSD_CHUNK
    if rev:
        cmap = lambda s: jnp.where(s < n_ctx_chunks, n_ctx_chunks - 1 - s, nc - 1 + n_ctx_chunks - s)
    else:
        cmap = lambda s: s
    nb_x = SSD_WIDTH // BC_WIDTH
    return pl.pallas_call(
        functools.partial(_ssd_kernel, rev=rev),
        grid=(bsz, nc),
        in_specs=[
            pl.BlockSpec((1, SSD_CHUNK, SSD_WIDTH), lambda b, s: (b, cmap(s), 0)),
            pl.BlockSpec((1, SSD_CHUNK, BC_WIDTH), lambda b, s: (b, cmap(s), nb_x)),
            pl.BlockSpec((1, SSD_CHUNK, BC_WIDTH), lambda b, s: (b, cmap(s), nb_x + 1)),
            pl.BlockSpec((1, SSD_CHUNK, LANE), lambda b, s: (b, cmap(s), 0)),
            pl.BlockSpec((1, LANE), lambda b, s: (0, 0)),
            pl.BlockSpec((1, LANE), lambda b, s: (0, 0)),
        ],
        out_specs=pl.BlockSpec((1, SSD_CHUNK, SSD_WIDTH), lambda b, s: (b, cmap(s), 0)),
        out_shape=jax.ShapeDtypeStruct((bsz, nt, SSD_WIDTH), F32),
        scratch_shapes=[pltpu.VMEM((SSD_GROUPS, SSD_STATE, SSD_GROUP_WIDTH), F32)],
        compiler_params=_cparams(("arbitrary", "arbitrary")),
        name="ssd_scan_bwd" if rev else "ssd_scan_fwd",
    )(xbc, xbc, xbc, dt_raw, dt_bias, a_log)


def _mixer_merge_kernel(yf_ref, yb_ref, xs_ref, z_ref, u_ref, v_ref, dsk_ref, sg_ref, vg_ref,
                        ws_ref, bst_ref, o_ref):
    y = yf_ref[...] + yb_ref[...] + xs_ref[...] * dsk_ref[...]
    y_ssd = _norm_rows(y * jax.nn.silu(z_ref[...]), sg_ref[...])
    o_ref[:, :SSD_WIDTH] = y_ssd.astype(o_ref.dtype)
    u = jax.nn.gelu(u_ref[...])
    v = _norm_rows(jax.nn.gelu(v_ref[...]), vg_ref[...]).astype(BF16)
    for g in range(GM_GROUPS):
        sl = slice(g * GM_GROUP_DIM, (g + 1) * GM_GROUP_DIM)
        s = jnp.dot(ws_ref[g], v[:, sl], preferred_element_type=F32) + bst_ref[:, g:g + 1]
        o_ref[:, SSD_WIDTH + g * GM_GROUP_DIM:SSD_WIDTH + (g + 1) * GM_GROUP_DIM] = (u[:, sl] * s).astype(o_ref.dtype)


def _mixer_merge(yf, yb, xbc, proj, d_exp, ssd_g, v_g, w_s, b_s_t):
    rows = yf.shape[0]
    tm = GM_CHUNK
    row = lambda c: (lambda i: (i, c))
    const2 = lambda i: (0, 0)
    return pl.pallas_call(
        _mixer_merge_kernel,
        grid=(rows // tm,),
        in_specs=[
            pl.BlockSpec((tm, SSD_WIDTH), row(0)),
            pl.BlockSpec((tm, SSD_WIDTH), row(0)),
            pl.BlockSpec((tm, SSD_WIDTH), row(0)),
            pl.BlockSpec((tm, SSD_WIDTH), row(0)),
            pl.BlockSpec((tm, GM_WIDTH), row(1)),
            pl.BlockSpec((tm, GM_WIDTH), row(2)),
            pl.BlockSpec((1, SSD_WIDTH), const2),
            pl.BlockSpec((1, SSD_WIDTH), const2),
            pl.BlockSpec((1, GM_WIDTH), const2),
            pl.BlockSpec((GM_GROUPS, GM_CHUNK, GM_CHUNK), lambda i: (0, 0, 0)),
            pl.BlockSpec((GM_CHUNK, GM_GROUPS), const2),
        ],
        out_specs=pl.BlockSpec((tm, SSD_WIDTH + GM_WIDTH), lambda i: (i, 0)),
        out_shape=jax.ShapeDtypeStruct((rows, SSD_WIDTH + GM_WIDTH), BF16),
        compiler_params=_cparams(("arbitrary",)),
        name="mixer_merge",
    )(yf, yb, xbc, proj, proj, proj, d_exp, ssd_g, v_g, w_s, b_s_t)


def _attn_kernel(q_ref, kn_ref, kr_ref, v_ref, o_ref, kcat_ref, *, tk, n_kv):
    @pl.when(pl.program_id(2) == 0)
    def _():
        kcat_ref[:, :QK_NOPE] = kn_ref[0]
        kcat_ref[:, QK_NOPE:] = kr_ref[0]

    q = q_ref[0]
    tq = q.shape[0]

    def body(c, carry):
        m, l, acc = carry
        off = pl.multiple_of(c * tk, tk)
        s = lax.dot_general(q, kcat_ref[pl.ds(off, tk), :], (((1,), (1,)), ((), ())),
                            preferred_element_type=F32)
        m_new = jnp.maximum(m, jnp.max(s, axis=-1, keepdims=True))
        alpha = jnp.exp2(m - m_new)
        p = jnp.exp2(s - m_new)
        l = alpha * l + jnp.sum(p, axis=-1, keepdims=True)
        acc = alpha * acc + jnp.dot(p.astype(BF16), v_ref[0, pl.ds(off, tk), :], preferred_element_type=F32)
        return m_new, l, acc

    init = (jnp.full((tq, 1), -jnp.inf, F32), jnp.zeros((tq, 1), F32), jnp.zeros((tq, V_DIM), F32))
    _, l, acc = lax.fori_loop(0, n_kv, body, init, unroll=True)
    o_ref[0] = (acc / l).astype(o_ref.dtype)


def _attention(q, kv, kr, *, tq, tk):
    bsz, nq, _ = q.shape
    nt = kv.shape[1]
    qk_w = QK_NOPE + 2 * QK_ROPE
    return pl.pallas_call(
        functools.partial(_attn_kernel, tk=tk, n_kv=nt // tk),
        grid=(bsz, MLA_HEADS, nq // tq),
        in_specs=[
            pl.BlockSpec((1, tq, qk_w), lambda b, h, i: (b, i, h)),
            pl.BlockSpec((1, nt, QK_NOPE), lambda b, h, i: (b, 0, 2 * h)),
            pl.BlockSpec((1, nt, 2 * QK_ROPE), lambda b, h, i: (b, 0, 0)),
            pl.BlockSpec((1, nt, V_DIM), lambda b, h, i: (b, 0, 2 * h + 1)),
        ],
        out_specs=pl.BlockSpec((1, tq, V_DIM), lambda b, h, i: (b, i, h)),
        out_shape=jax.ShapeDtypeStruct((bsz, nq, MLA_HEADS * V_DIM), BF16),
        scratch_shapes=[pltpu.VMEM((nt, qk_w), BF16)],
        compiler_params=_cparams(("arbitrary", "arbitrary", "arbitrary")),
        name="mla_attention",
    )(q, kv, kr, kv)


def _ffn_pre_kernel(x_ref, g_ref, mod_ref, rw_ref, rb_ref, h_ref, lg_ref, *, tm, rows_per_batch, n_ctx):
    i = pl.program_id(0)
    is_ctx = _is_ctx_rows(i, tm, rows_per_batch, n_ctx)
    h = _norm_rows(x_ref[...], g_ref[...])
    h = h * (1.0 + _mod_select(mod_ref, 4, is_ctx)) + _mod_select(mod_ref, 3, is_ctx)
    h_ref[...] = h.reshape(h_ref.shape).astype(h_ref.dtype)
    lg_ref[...] = jnp.dot(h, rw_ref[...], precision=HIGHEST, preferred_element_type=F32) + rb_ref[...]


def _ffn_pre(x, g, mod, rw_pad, rb_pad, *, tm, rows_per_batch, n_ctx):
    rows, d = x.shape
    batch_of = lambda i: (i * tm) // rows_per_batch
    return pl.pallas_call(
        functools.partial(_ffn_pre_kernel, tm=tm, rows_per_batch=rows_per_batch, n_ctx=n_ctx),
        grid=(rows // tm,),
        in_specs=[
            pl.BlockSpec((tm, d), lambda i: (i, 0)),
            pl.BlockSpec((1, d), lambda i: (0, 0)),
            pl.BlockSpec((1, 2, 6, d), lambda i: (batch_of(i), 0, 0, 0)),
            pl.BlockSpec((d, LANE), lambda i: (0, 0)),
            pl.BlockSpec((1, LANE), lambda i: (0, 0)),
        ],
        out_specs=[pl.BlockSpec((tm, d // LANE, LANE), lambda i: (i, 0, 0)),
                   pl.BlockSpec((tm, LANE), lambda i: (i, 0))],
        out_shape=[jax.ShapeDtypeStruct((rows, d // LANE, LANE), BF16), jax.ShapeDtypeStruct((rows, LANE), F32)],
        compiler_params=_cparams(("arbitrary",)),
        name="ffn_prenorm_router",
    )(x, g.reshape(1, d), mod, rw_pad, rb_pad)


def _row_copy(src_hbm, row, buf, slot, sem):
    return pltpu.make_async_copy(src_hbm.at[pl.ds(row, 1)], buf.at[pl.ds(slot, 1)], sem)


def _start_rows(src_hbm, ids_ref, n_rows, dst_row, buf, sem, row0=0):
    def body(h, c):
        for q in range(DMA_QUEUES):
            r = row0 + h * DMA_QUEUES + q
            _row_copy(src_hbm, ids_ref[0, 0, r], buf, dst_row(r), sem).start(priority=q)
        return c
    lax.fori_loop(0, n_rows // DMA_QUEUES, body, 0, unroll=4)


def _wait_rows(src_hbm, n_rows, buf, sem):
    def body(r, c):
        _row_copy(src_hbm, 0, buf, r, sem).wait()
        return c
    lax.fori_loop(0, n_rows, body, 0, unroll=8)


def _gather_rows_kernel(nv_ref, idx_ref, idx_next_ref, src_hbm, o_ref, buf, sem, *, bm):
    i = pl.program_id(0)
    nv = nv_ref[0]
    cur = i % 2
    same_row = lambda r: r

    @pl.when(i == 0)
    def _():
        _start_rows(src_hbm, idx_ref, bm, same_row, buf.at[0], sem.at[0])

    @pl.when(i + 1 < nv)
    def _():
        _start_rows(src_hbm, idx_next_ref, bm, same_row, buf.at[1 - cur], sem.at[1 - cur])

    @pl.when(i < nv)
    def _():
        _wait_rows(src_hbm, bm, buf.at[cur], sem.at[cur])
        o_ref[...] = buf[cur].astype(F32).reshape(o_ref.shape).astype(o_ref.dtype)

    @pl.when(i >= nv)
    def _():
        o_ref[...] = jnp.zeros_like(o_ref)


def _gather_rows(src, idx, n_valid, *, bm, out_dtype):
    n_slots = idx.shape[0]
    row_shape = src.shape[1:]
    d = row_shape[0] * row_shape[1]
    nb = n_slots // bm
    clamp = lambda i, nv: jnp.minimum(i, nv[0] - 1)
    idx3 = idx.reshape(nb, 1, bm)
    return pl.pallas_call(
        functools.partial(_gather_rows_kernel, bm=bm),
        grid_spec=pltpu.PrefetchScalarGridSpec(
            num_scalar_prefetch=1,
            grid=(nb,),
            in_specs=[
                pl.BlockSpec((1, 1, bm), lambda i, nv: (clamp(i, nv), 0, 0), memory_space=pltpu.SMEM),
                pl.BlockSpec((1, 1, bm), lambda i, nv: (clamp(i + 1, nv), 0, 0), memory_space=pltpu.SMEM),
                pl.BlockSpec(memory_space=pl.ANY),
            ],
            out_specs=pl.BlockSpec((bm, d), lambda i, nv: (i, 0)),
            scratch_shapes=[pltpu.VMEM((2, bm) + row_shape, src.dtype), pltpu.SemaphoreType.DMA((2,))],
        ),
        out_shape=jax.ShapeDtypeStruct((n_slots, d), out_dtype),
        compiler_params=_cparams(("arbitrary",)),
        name="moe_gather_rows",
    )(n_valid, idx3, idx3, src)


DEINT_CHUNK = 2 * LANE


def _deinterleave_perm():
    k_i = lax.broadcasted_iota(jnp.int32, (DEINT_CHUNK, DEINT_CHUNK), 0)
    n_i = lax.broadcasted_iota(jnp.int32, (DEINT_CHUNK, DEINT_CHUNK), 1)
    return (k_i == jnp.where(n_i < LANE, 2 * n_i, 2 * (n_i - LANE) + 1)).astype(BF16)


def _expert_changed(be_ref, i):
    return (i == 0) | (be_ref[i] != be_ref[jnp.maximum(i - 1, 0)])


def _run_weights(w_hbm, wbuf, wsem, be_ref, run_ref, nxt_ref, i, layer_off):
    width = wbuf.shape[2]
    col = pl.multiple_of(pl.program_id(0) * width, width)
    slot = run_ref[i] % 2

    def copy(expert, s):
        return pltpu.make_async_copy(w_hbm.at[layer_off + expert, :, pl.ds(col, width)], wbuf.at[s], wsem.at[s])

    @pl.when(i == 0)
    def _():
        copy(be_ref[i], slot).start()

    copy(be_ref[i], slot).wait()

    @pl.when(nxt_ref[i] >= 0)
    def _():
        copy(nxt_ref[i], 1 - slot).start(priority=1)

    return slot


def _moe_up_kernel(be_ref, nv_ref, run_ref, nxt_ref, x_ref, w_hbm, p_ref, bg_ref, bl_ref, o_ref,
                   wbuf, wsem, wg_ref, wl_ref, *, layer_off):
    i = pl.program_id(1)
    valid = i < nv_ref[0]

    @pl.when(valid & _expert_changed(be_ref, i))
    def _():
        slot = _run_weights(w_hbm, wbuf, wsem, be_ref, run_ref, nxt_ref, i, layer_off)
        perm = p_ref[...]
        for c in range(wbuf.shape[2] // DEINT_CHUNK):
            r = jnp.dot(wbuf[slot, :, c * DEINT_CHUNK:(c + 1) * DEINT_CHUNK].astype(BF16), perm,
                        preferred_element_type=F32)
            wg_ref[:, c * LANE:(c + 1) * LANE] = r[:, :LANE].astype(BF16)
            wl_ref[:, c * LANE:(c + 1) * LANE] = r[:, LANE:].astype(BF16)

    @pl.when(valid)
    def _():
        x = x_ref[...]
        a_g = jnp.dot(x, wg_ref[...], preferred_element_type=F32) + bg_ref[...]
        a_l = jnp.dot(x, wl_ref[...], preferred_element_type=F32) + bl_ref[...]
        glu = jnp.minimum(a_g, SWIGLU_LIMIT)
        lin = jnp.clip(a_l, -SWIGLU_LIMIT, SWIGLU_LIMIT)
        o_ref[...] = (glu * jax.nn.sigmoid(SWIGLU_ALPHA * glu) * (lin + 1.0)).astype(o_ref.dtype)

    @pl.when(jnp.logical_not(valid))
    def _():
        o_ref[...] = jnp.zeros_like(o_ref)


def _moe_up(xs, w1_all, layer, b_glu, b_lin, routing, *, tn):
    n_slots, d = xs.shape
    n_layers, n_e, _, f2 = w1_all.shape
    ff = f2 // 2
    nb = n_slots // MOE_ROWS
    blk = lambda i, nv: jnp.minimum(i, nv[0] - 1)
    bmap = lambda j, i, be, nv, run, nxt: (be[blk(i, nv)], 0, j)
    return pl.pallas_call(
        functools.partial(_moe_up_kernel, layer_off=layer * n_e),
        grid_spec=pltpu.PrefetchScalarGridSpec(
            num_scalar_prefetch=4,
            grid=(ff // tn, nb),
            in_specs=[
                pl.BlockSpec((MOE_ROWS, d), lambda j, i, be, nv, run, nxt: (blk(i, nv), 0)),
                pl.BlockSpec(memory_space=pl.ANY),
                pl.BlockSpec((DEINT_CHUNK, DEINT_CHUNK), lambda j, i, be, nv, run, nxt: (0, 0)),
                pl.BlockSpec((None, 1, tn), bmap),
                pl.BlockSpec((None, 1, tn), bmap),
            ],
            out_specs=pl.BlockSpec((MOE_ROWS, tn), lambda j, i, be, nv, run, nxt: (i, j)),
            scratch_shapes=[pltpu.VMEM((2, d, 2 * tn), F32), pltpu.SemaphoreType.DMA((2,)),
                            pltpu.VMEM((d, tn), BF16), pltpu.VMEM((d, tn), BF16)],
        ),
        out_shape=jax.ShapeDtypeStruct((n_slots, ff), BF16),
        compiler_params=_cparams(("arbitrary", "arbitrary")),
        name="moe_up",
    )(*routing, xs, w1_all.reshape(n_layers * n_e, d, f2), _deinterleave_perm(), b_glu, b_lin)


def _moe_down_kernel(be_ref, nv_ref, run_ref, nxt_ref, a_ref, w_hbm, b_ref, o_ref, wbuf, wsem, wb_ref, *,
                     layer_off):
    i = pl.program_id(1)
    valid = i < nv_ref[0]

    @pl.when(valid & _expert_changed(be_ref, i))
    def _():
        slot = _run_weights(w_hbm, wbuf, wsem, be_ref, run_ref, nxt_ref, i, layer_off)
        wb_ref[...] = wbuf[slot].astype(BF16)

    @pl.when(valid)
    def _():
        y = jnp.dot(a_ref[...], wb_ref[...], preferred_element_type=F32) + b_ref[...]
        o_ref[...] = y.reshape(o_ref.shape)

    @pl.when(jnp.logical_not(valid))
    def _():
        o_ref[...] = jnp.zeros_like(o_ref)


def _moe_down(act, w2_all, layer, b2, routing):
    n_slots, ff = act.shape
    n_layers, n_e, _, d = w2_all.shape
    nb = n_slots // MOE_ROWS
    blk = lambda i, nv: jnp.minimum(i, nv[0] - 1)
    return pl.pallas_call(
        functools.partial(_moe_down_kernel, layer_off=layer * n_e),
        grid_spec=pltpu.PrefetchScalarGridSpec(
            num_scalar_prefetch=4,
            grid=(1, nb),
            in_specs=[
                pl.BlockSpec((MOE_ROWS, ff), lambda j, i, be, nv, run, nxt: (blk(i, nv), 0)),
                pl.BlockSpec(memory_space=pl.ANY),
                pl.BlockSpec((None, 1, d), lambda j, i, be, nv, run, nxt: (be[blk(i, nv)], 0, 0)),
            ],
            out_specs=pl.BlockSpec((MOE_ROWS, d // LANE, LANE), lambda j, i, be, nv, run, nxt: (i, 0, 0)),
            scratch_shapes=[pltpu.VMEM((2, ff, d), F32), pltpu.SemaphoreType.DMA((2,)), pltpu.VMEM((ff, d), BF16)],
        ),
        out_shape=jax.ShapeDtypeStruct((n_slots, d // LANE, LANE), F32),
        compiler_params=_cparams(("arbitrary", "arbitrary")),
        name="moe_down",
    )(*routing, act, w2_all.reshape(n_layers * n_e, ff, d), b2)


def _moe_combine_kernel(pos_ref, pos_next_ref, y_hbm, gate_ref, x_ref, mod_ref, fg_ref, o_ref, buf, sem, *,
                        tm, rows_per_batch, n_ctx, final_norm):
    n_rows = TOP_K * tm
    i = pl.program_id(0)
    cur = i % 2
    choice_major = lambda r: (r % TOP_K) * tm + r // TOP_K

    @pl.when(i == 0)
    def _():
        _start_rows(y_hbm, pos_ref, n_rows, choice_major, buf.at[0], sem.at[0])

    _wait_rows(y_hbm, n_rows, buf.at[cur], sem.at[cur])

    gates = gate_ref[...]
    for kk in range(TOP_K):
        @pl.when(i + 1 < pl.num_programs(0))
        def _():
            _start_rows(y_hbm, pos_next_ref, tm, choice_major, buf.at[1 - cur], sem.at[1 - cur], row0=kk * tm)

        term = gates[:, kk:kk + 1] * buf[cur, kk * tm:(kk + 1) * tm].reshape(x_ref.shape)
        if kk == 0:
            o_ref[...] = term
        else:
            o_ref[...] += term
    is_ctx = _is_ctx_rows(i, tm, rows_per_batch, n_ctx)
    out = x_ref[...] + _mod_select(mod_ref, 5, is_ctx) * o_ref[...]
    if final_norm:
        out = _norm_rows(out, fg_ref[...])
    o_ref[...] = out


def _moe_combine(y2, pos, gates, x, mod, final_g, *, tm, rows_per_batch, n_ctx, final_norm):
    rows, d = x.shape
    nblk = rows // tm
    batch_of = lambda i: (i * tm) // rows_per_batch
    pos3 = pos.reshape(nblk, 1, TOP_K * tm)
    return pl.pallas_call(
        functools.partial(_moe_combine_kernel, tm=tm, rows_per_batch=rows_per_batch, n_ctx=n_ctx,
                          final_norm=final_norm),
        grid=(nblk,),
        in_specs=[
            pl.BlockSpec((1, 1, TOP_K * tm), lambda i: (i, 0, 0), memory_space=pltpu.SMEM),
            pl.BlockSpec((1, 1, TOP_K * tm), lambda i: (jnp.minimum(i + 1, nblk - 1), 0, 0),
                         memory_space=pltpu.SMEM),
            pl.BlockSpec(memory_space=pl.ANY),
            pl.BlockSpec((tm, TOP_K), lambda i: (i, 0)),
            pl.BlockSpec((tm, d), lambda i: (i, 0)),
            pl.BlockSpec((1, 2, 6, d), lambda i: (batch_of(i), 0, 0, 0)),
            pl.BlockSpec((1, d), lambda i: (0, 0)),
        ],
        out_specs=pl.BlockSpec((tm, d), lambda i: (i, 0)),
        out_shape=jax.ShapeDtypeStruct((rows, d), F32),
        scratch_shapes=[pltpu.VMEM((2, TOP_K * tm) + y2.shape[1:], F32), pltpu.SemaphoreType.DMA((2,))],
        compiler_params=_cparams(("arbitrary",)),
        name="moe_combine",
    )(pos3, pos3, y2, gates, x, mod, final_g.reshape(1, d))


def _route(logits, n_experts):
    t_tok = logits.shape[0]
    top_val, top_idx = lax.top_k(logits, TOP_K)
    gates = jax.nn.softmax(top_val, axis=-1)
    flat_e = top_idx.reshape(-1).astype(jnp.int32)
    n_assign = t_tok * TOP_K
    experts = jnp.arange(n_experts, dtype=jnp.int32)
    counts = jnp.sum((flat_e[:, None] == experts[None, :]).astype(jnp.int32), axis=0)
    padded = (counts + MOE_ROWS - 1) // MOE_ROWS * MOE_ROWS
    incl = (experts[None, :] <= experts[:, None]).astype(jnp.int32)
    cum_pad = jnp.sum(incl * padded[None, :], axis=1)
    pad_start = cum_pad - padded
    start = jnp.sum(incl * counts[None, :], axis=1) - counts
    order = jnp.argsort(flat_e).astype(jnp.int32)
    rank = jnp.argsort(order).astype(jnp.int32)
    pos = pad_start[flat_e] + rank - start[flat_e]
    n_blocks = n_assign // MOE_ROWS + n_experts
    block_start = jnp.arange(n_blocks, dtype=jnp.int32) * MOE_ROWS
    block_expert = jnp.clip(jnp.sum((cum_pad[None, :] <= block_start[:, None]).astype(jnp.int32), axis=1),
                            0, n_experts - 1)
    slot_e = jnp.repeat(block_expert, MOE_ROWS)
    within = jnp.arange(n_blocks * MOE_ROWS, dtype=jnp.int32) - pad_start[slot_e]
    src = jnp.clip(start[slot_e] + within, 0, n_assign - 1)
    slot_tok = jnp.where(within < counts[slot_e], order[src] // TOP_K, 0)
    n_valid = cum_pad[-1:] // MOE_ROWS
    blocks = jnp.arange(n_blocks, dtype=jnp.int32)
    is_valid = blocks < n_valid[0]
    starts = is_valid & ((blocks == 0) | (block_expert != jnp.roll(block_expert, 1)))
    run_id = jnp.sum((starts[None, :] & (blocks[None, :] <= blocks[:, None])).astype(jnp.int32), axis=1) - 1
    later_start = starts[None, :] & (blocks[None, :] > blocks[:, None])
    next_start = jnp.min(jnp.where(later_start, blocks[None, :], n_blocks), axis=1)
    next_expert = jnp.where(next_start < n_blocks, block_expert[jnp.minimum(next_start, n_blocks - 1)], -1)
    return gates, slot_tok, pos, (block_expert, n_valid, jnp.maximum(run_id, 0), next_expert)


def _moe_layer(x, norm_g, mod, router_w, router_b, w1_all, b1, w2_all, b2, final_g, *,
               layer, rows_per_batch, n_ctx, tm_rows, final_norm):
    rows, d = x.shape
    n_experts = router_w.shape[1]
    ff = w2_all.shape[2]
    rw_pad = jnp.pad(router_w, ((0, 0), (0, LANE - n_experts)))
    rb_pad = jnp.pad(router_b, (0, LANE - n_experts)).reshape(1, LANE)
    hf, logits = _ffn_pre(x, norm_g, mod, rw_pad, rb_pad, tm=tm_rows, rows_per_batch=rows_per_batch, n_ctx=n_ctx)
    gates, slot_tok, pos, routing = _route(logits[:, :n_experts], n_experts)
    xs = _gather_rows(hf, slot_tok, routing[1], bm=MOE_ROWS, out_dtype=BF16)
    b_glu = b1[:, 0::2].reshape(n_experts, 1, ff)
    b_lin = b1[:, 1::2].reshape(n_experts, 1, ff)
    act = _moe_up(xs, w1_all, layer, b_glu, b_lin, routing, tn=_pick(ff, (1024, 512, 256, 128)))
    y2 = _moe_down(act, w2_all, layer, b2.reshape(n_experts, 1, d), routing)
    tm_c = _pick(rows_per_batch, (128,))
    return _moe_combine(y2, pos, gates, x, mod, final_g, tm=tm_c, rows_per_batch=rows_per_batch,
                        n_ctx=n_ctx, final_norm=final_norm)


def _modulation(c, c_ctx, ada_w, ada_b_i, layer):
    bsz, d = c.shape
    rows = 16
    s = jnp.concatenate([jax.nn.silu(c), jax.nn.silu(c_ctx)[None], jnp.zeros((rows - bsz - 1, d), F32)], axis=0)
    n = ada_w.shape[2]
    m = _fused_matmul(s, ada_w.reshape(-1, n), name="adaln_modulation", m_rows=rows, k=d, tm=rows,
                      tn=_pick(n, (1024, 512, 256, 128)), out_dtype=F32, rows_per_batch=rows,
                      w_row_block=layer, epilogue="bias", bias=ada_b_i)
    lat = m[:bsz].reshape(bsz, 6, d)
    ctx = jnp.broadcast_to(m[bsz].reshape(1, 6, d), (bsz, 6, d))
    return jnp.stack([ctx, lat], axis=1)


def _dwconv_silu(xbc, conv_w, conv_b, n_ctx):
    nt = xbc.shape[1]
    t = jnp.arange(nt)[None, :, None]
    prev = jnp.pad(xbc, ((0, 0), (1, 0), (0, 0)))[:, :nt]
    nxt = jnp.pad(xbc, ((0, 0), (0, 1), (0, 0)))[:, 1:]
    prev = jnp.where((t == 0) | (t == n_ctx), 0.0, prev)
    nxt = jnp.where((t == n_ctx - 1) | (t == nt - 1), 0.0, nxt)
    out = conv_b + prev * conv_w[0] + xbc * conv_w[1] + nxt * conv_w[2]
    return jax.nn.silu(out)


def _rope_tables(n_lat):
    rows = n_lat // GRID_W
    row = jnp.broadcast_to(jnp.arange(rows)[:, None], (rows, GRID_W)).reshape(-1)
    col = jnp.broadcast_to(jnp.arange(GRID_W)[None, :], (rows, GRID_W)).reshape(-1)
    freqs = ROPE_THETA ** (-jnp.arange(ROPE_PAIRS, dtype=F32) / ROPE_PAIRS)
    ang = jnp.stack([row[:, None] * freqs, col[:, None] * freqs], axis=1)
    cos, sin = jnp.cos(ang).astype(F32), jnp.sin(ang).astype(F32)
    c_tab = jnp.concatenate([cos[:, 0], cos[:, 0], cos[:, 1], cos[:, 1]], axis=-1)
    s_tab = jnp.concatenate([-sin[:, 0], sin[:, 0], -sin[:, 1], sin[:, 1]], axis=-1)
    return c_tab, s_tab


def _swap16(w):
    lane = jnp.arange(w.shape[-1])
    return jnp.take(w, jnp.where(lane % 32 < 16, lane + 16, lane - 16), axis=-1)


def kernel(x, c, ctx, c_ctx, mix_norm_g, ffn_norm_g, ada_w, ada_b, hyb_w_in, hyb_conv_w, hyb_conv_b, hyb_dt_bias, hyb_a_log, hyb_d_skip, hyb_ssd_norm_g, hyb_v_norm_g, hyb_w_s, hyb_b_s, hyb_w_out, mla_w_in, mla_q_norm_g, mla_kv_norm_g, mla_w_uq, mla_w_ukv, mla_w_o, router_w, router_b, exp_w1, exp_b1, exp_w2, exp_b2, final_norm_g):
    bsz, n_lat, d = x.shape
    n_ctx = ctx.shape[1]
    nt = n_ctx + n_lat
    rows = bsz * nt
    tm_s = _pick(nt, (768, 384, 256, 128))
    tm_l = _pick(n_lat, (1024, 512, 256, 128))

    stream = jnp.concatenate([ctx, x], axis=1).reshape(rows, d)

    mod0 = _modulation(c, c_ctx, ada_w, ada_b[0], 0)
    cuts = (SSD_WIDTH, SSD_WIDTH + XBC_WIDTH, SSD_WIDTH + XBC_WIDTH + 2 * SSD_HEADS)
    w_in = hyb_w_in[0]
    w_main = jnp.concatenate([w_in[:, :cuts[0]], w_in[:, cuts[2]:], w_in[:, cuts[0]:cuts[1]]], axis=1).astype(BF16)
    w_dt = jnp.pad(w_in[:, cuts[1]:cuts[2]], ((0, 0), (0, LANE - 2 * SSD_HEADS))).astype(BF16)
    pro = dict(prologue="norm_mod", g=mix_norm_g[0], mod=mod0, sh_idx=0, sc_idx=1,
               rows_per_batch=nt, n_ctx=n_ctx, m_rows=rows, k=d, tm=tm_s)
    proj = _fused_matmul(stream, w_main, name="hyb_in_proj", tn=_pick(w_main.shape[1], (1024, 512, 256, 128)), out_dtype=F32, **pro)
    dt_raw = _fused_matmul(stream, w_dt, name="hyb_dt_proj", tn=LANE, out_dtype=F32, **pro)

    xbc_off = SSD_WIDTH + 2 * GM_WIDTH
    xbc = _dwconv_silu(proj[:, xbc_off:].reshape(bsz, nt, XBC_WIDTH), hyb_conv_w[0], hyb_conv_b[0], n_ctx)
    dt_bias = jnp.pad(hyb_dt_bias[0].reshape(1, -1), ((0, 0), (0, LANE - 2 * SSD_HEADS)))
    a_log = jnp.pad(hyb_a_log[0].reshape(1, -1), ((0, 0), (0, LANE - 2 * SSD_HEADS)))
    dt3 = dt_raw.reshape(bsz, nt, LANE)
    n_ctx_chunks = n_ctx // SSD_CHUNK
    y_f = _ssd_scan(xbc, dt3, dt_bias, a_log, rev=False, n_ctx_chunks=n_ctx_chunks)
    y_b = _ssd_scan(xbc, dt3, dt_bias, a_log, rev=True, n_ctx_chunks=n_ctx_chunks)

    d_exp = jnp.repeat(hyb_d_skip[0], SSD_HEAD_DIM).reshape(1, SSD_WIDTH)
    y_cat = _mixer_merge(y_f.reshape(rows, SSD_WIDTH), y_b.reshape(rows, SSD_WIDTH),
                         xbc.reshape(rows, XBC_WIDTH), proj, d_exp,
                         hyb_ssd_norm_g[0].reshape(1, -1), hyb_v_norm_g[0].reshape(1, -1),
                         hyb_w_s[0].astype(BF16), hyb_b_s[0].T)
    stream = _fused_matmul(y_cat, hyb_w_out[0].astype(BF16), name="hyb_out_proj", m_rows=rows, k=SSD_WIDTH + GM_WIDTH, tm=tm_s,
                           tn=_pick(d, (1024, 512, 256, 128)), out_dtype=F32, rows_per_batch=nt, n_ctx=n_ctx,
                           epilogue="gated_resid", resid=stream, mod=mod0, gate_idx=2)
    stream = _moe_layer(stream, ffn_norm_g[0], mod0, router_w[0], router_b[0], exp_w1, exp_b1[0],
                        exp_w2, exp_b2[0], final_norm_g, layer=0, rows_per_batch=nt, n_ctx=n_ctx, tm_rows=tm_s,
                        final_norm=False)

    mod1 = _modulation(c, c_ctx, ada_w, ada_b[1], 1)
    w_in1 = mla_w_in[0]
    kpe_w = w_in1[:, Q_LORA + KV_LORA:]
    w_in1_r = jnp.concatenate([w_in1[:, Q_LORA:Q_LORA + KV_LORA], kpe_w, _swap16(kpe_w),
                               jnp.zeros((d, Q_LORA - KV_LORA - 2 * QK_ROPE), F32), w_in1[:, :Q_LORA]],
                              axis=1).astype(BF16)
    p1 = _fused_matmul(stream, w_in1_r, name="mla_in_proj", m_rows=rows, k=d, tm=tm_s, tn=w_in1_r.shape[1], out_dtype=F32,
                       prologue="norm_mod", g=mix_norm_g[1], mod=mod1, sh_idx=0, sc_idx=1,
                       rows_per_batch=nt, n_ctx=n_ctx)

    c_tab, s_tab = _rope_tables(n_lat)
    ck = jnp.concatenate([jnp.ones((n_ctx, QK_ROPE), F32), c_tab], axis=0)
    sk = jnp.concatenate([jnp.zeros((n_ctx, QK_ROPE), F32), s_tab], axis=0)
    p1_3 = p1.reshape(bsz, nt, -1)
    k_rot = p1_3[:, :, KV_LORA:KV_LORA + QK_ROPE] * ck + p1_3[:, :, KV_LORA + QK_ROPE:KV_LORA + 2 * QK_ROPE] * sk
    kr = jnp.concatenate([k_rot, k_rot], axis=-1).astype(BF16)

    kv = _fused_matmul(p1, mla_w_ukv[0].astype(BF16), name="mla_kv_up", m_rows=rows, k=KV_LORA, tm=tm_s,
                       tn=_pick(mla_w_ukv.shape[2], (2048, 1024, 512, 256, 128)), out_dtype=BF16,
                       rows_per_batch=nt, prologue="norm", g=mla_kv_norm_g[0], x_col_block=0)

    w_uq = mla_w_uq[0].reshape(Q_LORA, MLA_HEADS, QK_NOPE + QK_ROPE)
    w_q = jnp.concatenate([w_uq, _swap16(w_uq[:, :, QK_NOPE:])], axis=-1)
    w_q = w_q.reshape(Q_LORA, MLA_HEADS * (QK_NOPE + 2 * QK_ROPE)).astype(BF16)
    tm_q = _pick(n_ctx, (256, 128))
    lat_blocks = n_lat // tm_q
    q_rows = lambda i: (i // lat_blocks) * (nt // tm_q) + n_ctx // tm_q + i % lat_blocks
    q_tab = jnp.concatenate([jnp.ones((n_lat, QK_NOPE), F32), c_tab, s_tab], axis=-1) * (MLA_SCALE * math.log2(math.e))
    q = _fused_matmul(p1, w_q, name="mla_q_proj", m_rows=bsz * n_lat, k=Q_LORA, tm=tm_q, tn=w_q.shape[1],
                      out_dtype=BF16, rows_per_batch=n_lat, prologue="norm", g=mla_q_norm_g[0],
                      x_row_map=q_rows, x_col_block=1, epilogue="mul_table", table=q_tab,
                      table_row_map=lambda i: i % lat_blocks)

    o = _attention(q.reshape(bsz, n_lat, -1), kv.reshape(bsz, nt, -1), kr,
                   tq=_pick(n_lat, (1024, 512, 256, 128)), tk=_pick(nt, (2816, 768, 256, 128)))

    lat = stream.reshape(bsz, nt, d)[:, n_ctx:].reshape(bsz * n_lat, d)
    mod1_lat = mod1
    lat = _fused_matmul(o.reshape(bsz * n_lat, -1), mla_w_o[0].astype(BF16), name="mla_out_proj",
                        m_rows=bsz * n_lat,
                        k=MLA_HEADS * V_DIM, tm=tm_l, tn=_pick(d, (1024, 512, 256, 128)), out_dtype=F32,
                        rows_per_batch=n_lat, epilogue="gated_resid", resid=lat, mod=mod1_lat, gate_idx=2)
    out = _moe_layer(lat, ffn_norm_g[1], mod1_lat, router_w[1], router_b[1], exp_w1, exp_b1[1],
                     exp_w2, exp_b2[1], final_norm_g, layer=1, rows_per_batch=n_lat, n_ctx=0, tm_rows=tm_l,
                     final_norm=True)
    return out.reshape(bsz, n_lat, d)
```

```python
import functools
import math

import jax
import jax.numpy as jnp
from jax import lax
from jax.experimental import pallas as pl
from jax.experimental.pallas import tpu as pltpu

F32 = jnp.float32
BF16 = jnp.bfloat16
HIGHEST = lax.Precision.HIGHEST
EPS = 1e-6

GRID_W = 64
SSD_HEADS = 32
SSD_HEAD_DIM = 64
SSD_WIDTH = SSD_HEADS * SSD_HEAD_DIM
SSD_GROUPS = 4
SSD_HEADS_PER_GROUP = SSD_HEADS // SSD_GROUPS
SSD_STATE = 128
SSD_CHUNK = 128
SSD_GROUP_WIDTH = SSD_HEADS_PER_GROUP * SSD_HEAD_DIM
BC_WIDTH = SSD_GROUPS * SSD_STATE
XBC_WIDTH = SSD_WIDTH + 2 * BC_WIDTH
GM_GROUPS = 16
GM_GROUP_DIM = 128
GM_WIDTH = GM_GROUPS * GM_GROUP_DIM
GM_CHUNK = 128
MLA_HEADS = 16
Q_LORA = 768
KV_LORA = 512
QK_NOPE = 128
QK_ROPE = 64
V_DIM = 128
ROPE_PAIRS = QK_ROPE // 4
ROPE_THETA = 10000.0
MLA_SCALE = (QK_NOPE + QK_ROPE) ** -0.5
TOP_K = 4
SWIGLU_LIMIT = 7.0
SWIGLU_ALPHA = 1.702
MOE_ROWS = 256

LANE = 128
DMA_QUEUES = 2
VMEM_LIMIT = 56 * 1024 * 1024


def _pick(n, candidates):
    for c in candidates:
        if n % c == 0:
            return c
    raise ValueError(f"no tile for {n} in {candidates}")


def _cparams(sem):
    return pltpu.CompilerParams(dimension_semantics=sem, vmem_limit_bytes=VMEM_LIMIT)


def _is_ctx_rows(i, tm, rows_per_batch, n_ctx):
    row = (i * tm) % rows_per_batch + lax.broadcasted_iota(jnp.int32, (tm, 1), 0)
    return row < n_ctx


def _mod_select(mod_ref, idx, is_ctx):
    return jnp.where(is_ctx, mod_ref[0, 0, idx:idx + 1, :], mod_ref[0, 1, idx:idx + 1, :])


def _norm_rows(x, g):
    y = x * lax.rsqrt(jnp.mean(x * x, axis=-1, keepdims=True) + EPS)
    return y * g


def _fused_matmul_kernel(*refs, prologue, epilogue, tm, tn, rows_per_batch, n_ctx,
                         sh_idx, sc_idx, gate_idx, use_scratch):
    it = iter(refs)
    x_ref = next(it)
    g_ref = next(it) if prologue != "none" else None
    pmod_ref = next(it) if prologue == "norm_mod" else None
    w_ref = next(it)
    bias_ref = next(it) if epilogue == "bias" else None
    resid_ref = next(it) if epilogue == "gated_resid" else None
    emod_ref = next(it) if epilogue == "gated_resid" else None
    tab_ref = next(it) if epilogue == "mul_table" else None
    o_ref = next(it)
    xn_ref = next(it) if use_scratch else None

    i = pl.program_id(0)
    j = pl.program_id(1)

    if use_scratch:
        @pl.when(j == 0)
        def _():
            x = x_ref[...].astype(F32)
            if prologue != "none":
                x = _norm_rows(x, g_ref[...])
            if prologue == "norm_mod":
                is_ctx = _is_ctx_rows(i, tm, rows_per_batch, n_ctx)
                x = x * (1.0 + _mod_select(pmod_ref, sc_idx, is_ctx)) + _mod_select(pmod_ref, sh_idx, is_ctx)
            xn_ref[...] = x.astype(BF16)
        xn = xn_ref[...]
    else:
        xn = x_ref[...]

    acc = jnp.dot(xn, w_ref[...].astype(BF16), preferred_element_type=F32)
    if epilogue == "bias":
        acc = acc + bias_ref[...]
    elif epilogue == "gated_resid":
        is_ctx = _is_ctx_rows(i, tm, rows_per_batch, n_ctx)
        acc = resid_ref[...] + _mod_select(emod_ref, gate_idx, is_ctx) * acc
    elif epilogue == "mul_table":
        acc = acc * jnp.tile(tab_ref[...], (1, tn // tab_ref.shape[1]))
    o_ref[...] = acc.astype(o_ref.dtype)


def _fused_matmul(x, w, *, name, m_rows, k, tm, tn, out_dtype, rows_per_batch, n_ctx=0,
                  prologue="none", g=None, mod=None, sh_idx=0, sc_idx=0,
                  epilogue="none", bias=None, resid=None, gate_idx=0, table=None,
                  x_row_map=None, x_col_block=0, w_row_block=0, resid_row_map=None, table_row_map=None):
    n = w.shape[1]
    assert w.shape[0] % k == 0 and m_rows % tm == 0 and n % tn == 0 and rows_per_batch % tm == 0
    ident = lambda i: i
    x_row_map = x_row_map or ident
    resid_row_map = resid_row_map or ident
    table_row_map = table_row_map or ident
    batch_of = lambda i: (i * tm) // rows_per_batch
    use_scratch = prologue != "none" or x.dtype != BF16

    args = [x]
    in_specs = [pl.BlockSpec((tm, k), lambda i, j: (x_row_map(i), x_col_block))]
    if prologue != "none":
        args.append(g.reshape(1, k))
        in_specs.append(pl.BlockSpec((1, k), lambda i, j: (0, 0)))
    if prologue == "norm_mod":
        args.append(mod)
        in_specs.append(pl.BlockSpec((1, 2, 6, k), lambda i, j: (batch_of(i), 0, 0, 0)))
    args.append(w)
    in_specs.append(pl.BlockSpec((k, tn), lambda i, j: (w_row_block, j)))
    if epilogue == "bias":
        args.append(bias.reshape(1, n))
        in_specs.append(pl.BlockSpec((1, tn), lambda i, j: (0, j)))
    if epilogue == "gated_resid":
        args.append(resid)
        in_specs.append(pl.BlockSpec((tm, tn), lambda i, j: (resid_row_map(i), j)))
        args.append(mod)
        in_specs.append(pl.BlockSpec((1, 2, 6, tn), lambda i, j: (batch_of(i), 0, 0, j)))
    if epilogue == "mul_table":
        args.append(table)
        in_specs.append(pl.BlockSpec((tm, table.shape[1]), lambda i, j: (table_row_map(i), 0)))

    kern = functools.partial(
        _fused_matmul_kernel, prologue=prologue, epilogue=epilogue, tm=tm, tn=tn,
        rows_per_batch=rows_per_batch, n_ctx=n_ctx, sh_idx=sh_idx, sc_idx=sc_idx,
        gate_idx=gate_idx, use_scratch=use_scratch)
    return pl.pallas_call(
        kern,
        grid=(m_rows // tm, n // tn),
        in_specs=in_specs,
        out_specs=pl.BlockSpec((tm, tn), lambda i, j: (i, j)),
        out_shape=jax.ShapeDtypeStruct((m_rows, n), out_dtype),
        scratch_shapes=[pltpu.VMEM((tm, k), BF16)] if use_scratch else [],
        compiler_params=_cparams(("arbitrary", "arbitrary")),
        name=name,
    )(*args)


def _expand_heads(mat, col0):
    rows = mat.shape[0]
    return jnp.concatenate(
        [jnp.broadcast_to(mat[:, col0 + jj:col0 + jj + 1], (rows, SSD_HEAD_DIM))
         for jj in range(SSD_HEADS_PER_GROUP)], axis=1)


def _bf16_split3(mat):
    hi = mat.astype(BF16)
    r1 = mat - hi.astype(F32)
    mid = r1.astype(BF16)
    lo = (r1 - mid.astype(F32)).astype(BF16)
    return jnp.concatenate([hi, mid, lo], axis=1)


def _head_expansion(rev):
    k_i = lax.broadcasted_iota(jnp.int32, (SSD_GROUPS, 3 * LANE, SSD_GROUP_WIDTH), 1) % LANE
    n_i = lax.broadcasted_iota(jnp.int32, (SSD_GROUPS, 3 * LANE, SSD_GROUP_WIDTH), 2) // SSD_HEAD_DIM
    g_i = lax.broadcasted_iota(jnp.int32, (SSD_GROUPS, 3 * LANE, SSD_GROUP_WIDTH), 0)
    dcol = SSD_HEADS if rev else 0
    return (k_i == dcol + g_i * SSD_HEADS_PER_GROUP + n_i).astype(BF16)


def _ssd_kernel(xs_ref, b_ref, c_ref, dtr_ref, dtb_ref, alog_ref, e_ref, y_ref, h_ref, *, rev):
    L = SSD_CHUNK
    step = pl.program_id(1)

    @pl.when(step == 0)
    def _():
        h_ref[...] = jnp.zeros_like(h_ref)

    dcol = SSD_HEADS if rev else 0
    dt = jax.nn.softplus(dtr_ref[0] + dtb_ref[...])
    a = dt * (-jnp.exp(alog_ref[...]))
    r_i = lax.broadcasted_iota(jnp.int32, (L, L), 0)
    c_i = lax.broadcasted_iota(jnp.int32, (L, L), 1)
    tri = (r_i <= c_i) if rev else (r_i >= c_i)
    acs = jnp.dot(tri.astype(F32), a, precision=HIGHEST, preferred_element_type=F32)
    acs_t = acs.T
    edge = acs[0:1, :] if rev else acs[L - 1:L, :]
    decay_end = jnp.exp(edge - acs)
    exp_acs = jnp.exp(acs)
    chunk_decay = jnp.exp(edge)
    dt3, exp_acs3, decay_end3 = _bf16_split3(dt), _bf16_split3(exp_acs), _bf16_split3(decay_end)
    spread = lambda pieces, g: jnp.dot(pieces, e_ref[g], preferred_element_type=F32)

    for g in range(SSD_GROUPS):
        col0 = dcol + g * SSD_HEADS_PER_GROUP
        bg = b_ref[0, :, g * SSD_STATE:(g + 1) * SSD_STATE]
        cg = c_ref[0, :, g * SSD_STATE:(g + 1) * SSD_STATE].astype(BF16)
        cb = lax.dot_general(cg, bg.astype(BF16), (((1,), (1,)), ((), ())), preferred_element_type=F32)
        xdt = xs_ref[0, :, g * SSD_GROUP_WIDTH:(g + 1) * SSD_GROUP_WIDTH] * spread(dt3, g)
        y_diag = []
        for jj in range(SSD_HEADS_PER_GROUP):
            col = col0 + jj
            seg = acs[:, col:col + 1] - acs_t[col:col + 1, :]
            lmat = jnp.exp(jnp.where(tri, seg, -jnp.inf))
            y_diag.append(jnp.dot((cb * lmat).astype(BF16),
                                  xdt[:, jj * SSD_HEAD_DIM:(jj + 1) * SSD_HEAD_DIM].astype(BF16),
                                  preferred_element_type=F32))
        y_diag = jnp.concatenate(y_diag, axis=1)
        h_g = h_ref[g]
        y_off = jnp.dot(cg, h_g.astype(BF16), preferred_element_type=F32) * spread(exp_acs3, g)
        y_ref[0, :, g * SSD_GROUP_WIDTH:(g + 1) * SSD_GROUP_WIDTH] = y_diag + y_off
        states = jnp.dot(bg.T.astype(BF16), (xdt * spread(decay_end3, g)).astype(BF16),
                         preferred_element_type=F32)
        h_ref[g] = h_g * _expand_heads(chunk_decay, col0) + states


def _ssd_scan(xbc, dt_raw, dt_bias, a_log, *, rev, n_ctx_chunks):
    bsz, nt, _ = xbc.shape
    nc = nt // SSD_CHUNK
    if rev:
        cmap = lambda s: jnp.where(s < n_ctx_chunks, n_ctx_chunks - 1 - s, nc - 1 + n_ctx_chunks - s)
    else:
        cmap = lambda s: s
    nb_x = SSD_WIDTH // BC_WIDTH
    return pl.pallas_call(
        functools.partial(_ssd_kernel, rev=rev),
        grid=(bsz, nc),
        in_specs=[
            pl.BlockSpec((1, SSD_CHUNK, SSD_WIDTH), lambda b, s: (b, cmap(s), 0)),
            pl.BlockSpec((1, SSD_CHUNK, BC_WIDTH), lambda b, s: (b, cmap(s), nb_x)),
            pl.BlockSpec((1, SSD_CHUNK, BC_WIDTH), lambda b, s: (b, cmap(s), nb_x + 1)),
            pl.BlockSpec((1, SSD_CHUNK, LANE), lambda b, s: (b, cmap(s), 0)),
            pl.BlockSpec((1, LANE), lambda b, s: (0, 0)),
            pl.BlockSpec((1, LANE), lambda b, s: (0, 0)),
            pl.BlockSpec((SSD_GROUPS, 3 * LANE, SSD_GROUP_WIDTH), lambda b, s: (0, 0, 0)),
        ],
        out_specs=pl.BlockSpec((1, SSD_CHUNK, SSD_WIDTH), lambda b, s: (b, cmap(s), 0)),
        out_shape=jax.ShapeDtypeStruct((bsz, nt, SSD_WIDTH), F32),
        scratch_shapes=[pltpu.VMEM((SSD_GROUPS, SSD_STATE, SSD_GROUP_WIDTH), F32)],
        compiler_params=_cparams(("arbitrary", "arbitrary")),
        name="ssd_scan_bwd" if rev else "ssd_scan_fwd",
    )(xbc, xbc, xbc, dt_raw, dt_bias, a_log, _head_expansion(rev))


def _mixer_merge_kernel(yf_ref, yb_ref, xs_ref, z_ref, u_ref, v_ref, dsk_ref, sg_ref, vg_ref,
                        ws_ref, bst_ref, o_ref):
    y = yf_ref[...] + yb_ref[...] + xs_ref[...] * dsk_ref[...]
    y_ssd = _norm_rows(y * jax.nn.silu(z_ref[...]), sg_ref[...])
    o_ref[:, :SSD_WIDTH] = y_ssd.astype(o_ref.dtype)
    u = jax.nn.gelu(u_ref[...])
    v = _norm_rows(jax.nn.gelu(v_ref[...]), vg_ref[...]).astype(BF16)
    for g in range(GM_GROUPS):
        sl = slice(g * GM_GROUP_DIM, (g + 1) * GM_GROUP_DIM)
        s = jnp.dot(ws_ref[g], v[:, sl], preferred_element_type=F32) + bst_ref[:, g:g + 1]
        o_ref[:, SSD_WIDTH + g * GM_GROUP_DIM:SSD_WIDTH + (g + 1) * GM_GROUP_DIM] = (u[:, sl] * s).astype(o_ref.dtype)


def _mixer_merge(yf, yb, xbc, proj, d_exp, ssd_g, v_g, w_s, b_s_t):
    rows = yf.shape[0]
    tm = GM_CHUNK
    row = lambda c: (lambda i: (i, c))
    const2 = lambda i: (0, 0)
    return pl.pallas_call(
        _mixer_merge_kernel,
        grid=(rows // tm,),
        in_specs=[
            pl.BlockSpec((tm, SSD_WIDTH), row(0)),
            pl.BlockSpec((tm, SSD_WIDTH), row(0)),
            pl.BlockSpec((tm, SSD_WIDTH), row(0)),
            pl.BlockSpec((tm, SSD_WIDTH), row(0)),
            pl.BlockSpec((tm, GM_WIDTH), row(1)),
            pl.BlockSpec((tm, GM_WIDTH), row(2)),
            pl.BlockSpec((1, SSD_WIDTH), const2),
            pl.BlockSpec((1, SSD_WIDTH), const2),
            pl.BlockSpec((1, GM_WIDTH), const2),
            pl.BlockSpec((GM_GROUPS, GM_CHUNK, GM_CHUNK), lambda i: (0, 0, 0)),
            pl.BlockSpec((GM_CHUNK, GM_GROUPS), const2),
        ],
        out_specs=pl.BlockSpec((tm, SSD_WIDTH + GM_WIDTH), lambda i: (i, 0)),
        out_shape=jax.ShapeDtypeStruct((rows, SSD_WIDTH + GM_WIDTH), BF16),
        compiler_params=_cparams(("arbitrary",)),
        name="mixer_merge",
    )(yf, yb, xbc, proj, proj, proj, d_exp, ssd_g, v_g, w_s, b_s_t)


def _attn_kernel(q_ref, kn_ref, kr_ref, v_ref, o_ref, kcat_ref, *, tk, n_kv):
    @pl.when(pl.program_id(2) == 0)
    def _():
        kcat_ref[:, :QK_NOPE] = kn_ref[0]
        kcat_ref[:, QK_NOPE:] = kr_ref[0]

    q = q_ref[0]
    tq = q.shape[0]

    def body(c, carry):
        m, l, acc = carry
        off = pl.multiple_of(c * tk, tk)
        s = lax.dot_general(q, kcat_ref[pl.ds(off, tk), :], (((1,), (1,)), ((), ())),
                            preferred_element_type=F32)
        m_new = jnp.maximum(m, jnp.max(s, axis=-1, keepdims=True))
        alpha = jnp.exp2(m - m_new)
        p = jnp.exp2(s - m_new)
        l = alpha * l + jnp.sum(p, axis=-1, keepdims=True)
        acc = alpha * acc + jnp.dot(p.astype(BF16), v_ref[0, pl.ds(off, tk), :], preferred_element_type=F32)
        return m_new, l, acc

    init = (jnp.full((tq, 1), -jnp.inf, F32), jnp.zeros((tq, 1), F32), jnp.zeros((tq, V_DIM), F32))
    _, l, acc = lax.fori_loop(0, n_kv, body, init, unroll=True)
    o_ref[0] = (acc / l).astype(o_ref.dtype)


def _attention(q, kv, kr, *, tq, tk):
    bsz, nq, _ = q.shape
    nt = kv.shape[1]
    qk_w = QK_NOPE + 2 * QK_ROPE
    return pl.pallas_call(
        functools.partial(_attn_kernel, tk=tk, n_kv=nt // tk),
        grid=(bsz, MLA_HEADS, nq // tq),
        in_specs=[
            pl.BlockSpec((1, tq, qk_w), lambda b, h, i: (b, i, h)),
            pl.BlockSpec((1, nt, QK_NOPE), lambda b, h, i: (b, 0, 2 * h)),
            pl.BlockSpec((1, nt, 2 * QK_ROPE), lambda b, h, i: (b, 0, 0)),
            pl.BlockSpec((1, nt, V_DIM), lambda b, h, i: (b, 0, 2 * h + 1)),
        ],
        out_specs=pl.BlockSpec((1, tq, V_DIM), lambda b, h, i: (b, i, h)),
        out_shape=jax.ShapeDtypeStruct((bsz, nq, MLA_HEADS * V_DIM), BF16),
        scratch_shapes=[pltpu.VMEM((nt, qk_w), BF16)],
        compiler_params=_cparams(("arbitrary", "arbitrary", "arbitrary")),
        name="mla_attention",
    )(q, kv, kr, kv)


def _ffn_pre_kernel(x_ref, g_ref, mod_ref, rw_ref, rb_ref, h_ref, lg_ref, *, tm, rows_per_batch, n_ctx):
    i = pl.program_id(0)
    is_ctx = _is_ctx_rows(i, tm, rows_per_batch, n_ctx)
    h = _norm_rows(x_ref[...], g_ref[...])
    h = h * (1.0 + _mod_select(mod_ref, 4, is_ctx)) + _mod_select(mod_ref, 3, is_ctx)
    h_ref[...] = h.reshape(h_ref.shape).astype(h_ref.dtype)
    lg_ref[...] = jnp.dot(h, rw_ref[...], precision=HIGHEST, preferred_element_type=F32) + rb_ref[...]


def _ffn_pre(x, g, mod, rw_pad, rb_pad, *, tm, rows_per_batch, n_ctx):
    rows, d = x.shape
    batch_of = lambda i: (i * tm) // rows_per_batch
    return pl.pallas_call(
        functools.partial(_ffn_pre_kernel, tm=tm, rows_per_batch=rows_per_batch, n_ctx=n_ctx),
        grid=(rows // tm,),
        in_specs=[
            pl.BlockSpec((tm, d), lambda i: (i, 0)),
            pl.BlockSpec((1, d), lambda i: (0, 0)),
            pl.BlockSpec((1, 2, 6, d), lambda i: (batch_of(i), 0, 0, 0)),
            pl.BlockSpec((d, LANE), lambda i: (0, 0)),
            pl.BlockSpec((1, LANE), lambda i: (0, 0)),
        ],
        out_specs=[pl.BlockSpec((tm, d // LANE, LANE), lambda i: (i, 0, 0)),
                   pl.BlockSpec((tm, LANE), lambda i: (i, 0))],
        out_shape=[jax.ShapeDtypeStruct((rows, d // LANE, LANE), BF16), jax.ShapeDtypeStruct((rows, LANE), F32)],
        compiler_params=_cparams(("arbitrary",)),
        name="ffn_prenorm_router",
    )(x, g.reshape(1, d), mod, rw_pad, rb_pad)


def _row_copy(src_hbm, row, buf, slot, sem):
    return pltpu.make_async_copy(src_hbm.at[pl.ds(row, 1)], buf.at[pl.ds(slot, 1)], sem)


def _start_rows(src_hbm, ids_ref, n_rows, dst_row, buf, sem, row0=0):
    def body(h, c):
        for q in range(DMA_QUEUES):
            r = row0 + h * DMA_QUEUES + q
            _row_copy(src_hbm, ids_ref[0, 0, r], buf, dst_row(r), sem).start(priority=q)
        return c
    lax.fori_loop(0, n_rows // DMA_QUEUES, body, 0, unroll=4)


def _wait_rows(src_hbm, n_rows, buf, sem):
    def body(r, c):
        _row_copy(src_hbm, 0, buf, r, sem).wait()
        return c
    lax.fori_loop(0, n_rows, body, 0, unroll=8)


def _gather_rows_kernel(nv_ref, idx_ref, idx_next_ref, src_hbm, o_ref, buf, sem, *, bm):
    i = pl.program_id(0)
    nv = nv_ref[0]
    cur = i % 2
    same_row = lambda r: r

    @pl.when(i == 0)
    def _():
        _start_rows(src_hbm, idx_ref, bm, same_row, buf.at[0], sem.at[0])

    @pl.when(i + 1 < nv)
    def _():
        _start_rows(src_hbm, idx_next_ref, bm, same_row, buf.at[1 - cur], sem.at[1 - cur])

    @pl.when(i < nv)
    def _():
        _wait_rows(src_hbm, bm, buf.at[cur], sem.at[cur])
        o_ref[...] = buf[cur].astype(F32).reshape(o_ref.shape).astype(o_ref.dtype)

    @pl.when(i >= nv)
    def _():
        o_ref[...] = jnp.zeros_like(o_ref)


def _gather_rows(src, idx, n_valid, *, bm, out_dtype):
    n_slots = idx.shape[0]
    row_shape = src.shape[1:]
    d = row_shape[0] * row_shape[1]
    nb = n_slots // bm
    clamp = lambda i, nv: jnp.minimum(i, nv[0] - 1)
    idx3 = idx.reshape(nb, 1, bm)
    return pl.pallas_call(
        functools.partial(_gather_rows_kernel, bm=bm),
        grid_spec=pltpu.PrefetchScalarGridSpec(
            num_scalar_prefetch=1,
            grid=(nb,),
            in_specs=[
                pl.BlockSpec((1, 1, bm), lambda i, nv: (clamp(i, nv), 0, 0), memory_space=pltpu.SMEM),
                pl.BlockSpec((1, 1, bm), lambda i, nv: (clamp(i + 1, nv), 0, 0), memory_space=pltpu.SMEM),
                pl.BlockSpec(memory_space=pl.ANY),
            ],
            out_specs=pl.BlockSpec((bm, d), lambda i, nv: (i, 0)),
            scratch_shapes=[pltpu.VMEM((2, bm) + row_shape, src.dtype), pltpu.SemaphoreType.DMA((2,))],
        ),
        out_shape=jax.ShapeDtypeStruct((n_slots, d), out_dtype),
        compiler_params=_cparams(("arbitrary",)),
        name="moe_gather_rows",
    )(n_valid, idx3, idx3, src)


DEINT_CHUNK = 2 * LANE


def _deinterleave_perm():
    k_i = lax.broadcasted_iota(jnp.int32, (DEINT_CHUNK, DEINT_CHUNK), 0)
    n_i = lax.broadcasted_iota(jnp.int32, (DEINT_CHUNK, DEINT_CHUNK), 1)
    return (k_i == jnp.where(n_i < LANE, 2 * n_i, 2 * (n_i - LANE) + 1)).astype(BF16)


def _expert_changed(be_ref, i):
    return (i == 0) | (be_ref[i] != be_ref[jnp.maximum(i - 1, 0)])


def _run_weights(w_hbm, wbuf, wsem, be_ref, run_ref, nxt_ref, i, layer_off):
    width = wbuf.shape[2]
    col = pl.multiple_of(pl.program_id(0) * width, width)
    slot = run_ref[i] % 2

    def copy(expert, s):
        return pltpu.make_async_copy(w_hbm.at[layer_off + expert, :, pl.ds(col, width)], wbuf.at[s], wsem.at[s])

    @pl.when(i == 0)
    def _():
        copy(be_ref[i], slot).start()

    copy(be_ref[i], slot).wait()

    @pl.when(nxt_ref[i] >= 0)
    def _():
        copy(nxt_ref[i], 1 - slot).start(priority=1)

    return slot


def _moe_up_kernel(be_ref, nv_ref, run_ref, nxt_ref, x_ref, w_hbm, p_ref, bg_ref, bl_ref, o_ref,
                   wbuf, wsem, wg_ref, wl_ref, *, layer_off):
    i = pl.program_id(1)
    valid = i < nv_ref[0]

    @pl.when(valid & _expert_changed(be_ref, i))
    def _():
        slot = _run_weights(w_hbm, wbuf, wsem, be_ref, run_ref, nxt_ref, i, layer_off)
        perm = p_ref[...]
        for c in range(wbuf.shape[2] // DEINT_CHUNK):
            r = jnp.dot(wbuf[slot, :, c * DEINT_CHUNK:(c + 1) * DEINT_CHUNK].astype(BF16), perm,
                        preferred_element_type=F32)
            wg_ref[:, c * LANE:(c + 1) * LANE] = r[:, :LANE].astype(BF16)
            wl_ref[:, c * LANE:(c + 1) * LANE] = r[:, LANE:].astype(BF16)

    @pl.when(valid)
    def _():
        x = x_ref[...]
        a_g = jnp.dot(x, wg_ref[...], preferred_element_type=F32) + bg_ref[...]
        a_l = jnp.dot(x, wl_ref[...], preferred_element_type=F32) + bl_ref[...]
        glu = jnp.minimum(a_g, SWIGLU_LIMIT)
        lin = jnp.clip(a_l, -SWIGLU_LIMIT, SWIGLU_LIMIT)
        o_ref[...] = (glu * jax.nn.sigmoid(SWIGLU_ALPHA * glu) * (lin + 1.0)).astype(o_ref.dtype)

    @pl.when(jnp.logical_not(valid))
    def _():
        o_ref[...] = jnp.zeros_like(o_ref)


def _moe_up(xs, w1_all, layer, b_glu, b_lin, routing, *, tn):
    n_slots, d = xs.shape
    n_layers, n_e, _, f2 = w1_all.shape
    ff = f2 // 2
    nb = n_slots // MOE_ROWS
    blk = lambda i, nv: jnp.minimum(i, nv[0] - 1)
    bmap = lambda j, i, be, nv, run, nxt: (be[blk(i, nv)], 0, j)
    return pl.pallas_call(
        functools.partial(_moe_up_kernel, layer_off=layer * n_e),
        grid_spec=pltpu.PrefetchScalarGridSpec(
            num_scalar_prefetch=4,
            grid=(ff // tn, nb),
            in_specs=[
                pl.BlockSpec((MOE_ROWS, d), lambda j, i, be, nv, run, nxt: (blk(i, nv), 0)),
                pl.BlockSpec(memory_space=pl.ANY),
                pl.BlockSpec((DEINT_CHUNK, DEINT_CHUNK), lambda j, i, be, nv, run, nxt: (0, 0)),
                pl.BlockSpec((None, 1, tn), bmap),
                pl.BlockSpec((None, 1, tn), bmap),
            ],
            out_specs=pl.BlockSpec((MOE_ROWS, tn), lambda j, i, be, nv, run, nxt: (i, j)),
            scratch_shapes=[pltpu.VMEM((2, d, 2 * tn), F32), pltpu.SemaphoreType.DMA((2,)),
                            pltpu.VMEM((d, tn), BF16), pltpu.VMEM((d, tn), BF16)],
        ),
        out_shape=jax.ShapeDtypeStruct((n_slots, ff), BF16),
        compiler_params=_cparams(("arbitrary", "arbitrary")),
        name="moe_up",
    )(*routing, xs, w1_all.reshape(n_layers * n_e, d, f2), _deinterleave_perm(), b_glu, b_lin)


def _moe_down_kernel(be_ref, nv_ref, run_ref, nxt_ref, a_ref, w_hbm, b_ref, o_ref, wbuf, wsem, wb_ref, *,
                     layer_off):
    i = pl.program_id(1)
    valid = i < nv_ref[0]

    @pl.when(valid & _expert_changed(be_ref, i))
    def _():
        slot = _run_weights(w_hbm, wbuf, wsem, be_ref, run_ref, nxt_ref, i, layer_off)
        wb_ref[...] = wbuf[slot].astype(BF16)

    @pl.when(valid)
    def _():
        y = jnp.dot(a_ref[...], wb_ref[...], preferred_element_type=F32) + b_ref[...]
        o_ref[...] = y.reshape(o_ref.shape)

    @pl.when(jnp.logical_not(valid))
    def _():
        o_ref[...] = jnp.zeros_like(o_ref)


def _moe_down(act, w2_all, layer, b2, routing):
    n_slots, ff = act.shape
    n_layers, n_e, _, d = w2_all.shape
    nb = n_slots // MOE_ROWS
    blk = lambda i, nv: jnp.minimum(i, nv[0] - 1)
    return pl.pallas_call(
        functools.partial(_moe_down_kernel, layer_off=layer * n_e),
        grid_spec=pltpu.PrefetchScalarGridSpec(
            num_scalar_prefetch=4,
            grid=(1, nb),
            in_specs=[
                pl.BlockSpec((MOE_ROWS, ff), lambda j, i, be, nv, run, nxt: (blk(i, nv), 0)),
                pl.BlockSpec(memory_space=pl.ANY),
                pl.BlockSpec((None, 1, d), lambda j, i, be, nv, run, nxt: (be[blk(i, nv)], 0, 0)),
            ],
            out_specs=pl.BlockSpec((MOE_ROWS, d // LANE, LANE), lambda j, i, be, nv, run, nxt: (i, 0, 0)),
            scratch_shapes=[pltpu.VMEM((2, ff, d), F32), pltpu.SemaphoreType.DMA((2,)), pltpu.VMEM((ff, d), BF16)],
        ),
        out_shape=jax.ShapeDtypeStruct((n_slots, d // LANE, LANE), F32),
        compiler_params=_cparams(("arbitrary", "arbitrary")),
        name="moe_down",
    )(*routing, act, w2_all.reshape(n_layers * n_e, ff, d), b2)


def _moe_combine_kernel(pos_ref, pos_next_ref, y_hbm, gate_ref, x_ref, mod_ref, fg_ref, o_ref, buf, sem, *,
                        tm, rows_per_batch, n_ctx, final_norm):
    n_rows = TOP_K * tm
    i = pl.program_id(0)
    cur = i % 2
    choice_major = lambda r: (r % TOP_K) * tm + r // TOP_K

    @pl.when(i == 0)
    def _():
        _start_rows(y_hbm, pos_ref, n_rows, choice_major, buf.at[0], sem.at[0])

    _wait_rows(y_hbm, n_rows, buf.at[cur], sem.at[cur])

    gates = gate_ref[...]
    for kk in range(TOP_K):
        @pl.when(i + 1 < pl.num_programs(0))
        def _():
            _start_rows(y_hbm, pos_next_ref, tm, choice_major, buf.at[1 - cur], sem.at[1 - cur], row0=kk * tm)

        term = gates[:, kk:kk + 1] * buf[cur, kk * tm:(kk + 1) * tm].reshape(x_ref.shape)
        if kk == 0:
            o_ref[...] = term
        else:
            o_ref[...] += term
    is_ctx = _is_ctx_rows(i, tm, rows_per_batch, n_ctx)
    out = x_ref[...] + _mod_select(mod_ref, 5, is_ctx) * o_ref[...]
    if final_norm:
        out = _norm_rows(out, fg_ref[...])
    o_ref[...] = out


def _moe_combine(y2, pos, gates, x, mod, final_g, *, tm, rows_per_batch, n_ctx, final_norm):
    rows, d = x.shape
    nblk = rows // tm
    batch_of = lambda i: (i * tm) // rows_per_batch
    pos3 = pos.reshape(nblk, 1, TOP_K * tm)
    return pl.pallas_call(
        functools.partial(_moe_combine_kernel, tm=tm, rows_per_batch=rows_per_batch, n_ctx=n_ctx,
                          final_norm=final_norm),
        grid=(nblk,),
        in_specs=[
            pl.BlockSpec((1, 1, TOP_K * tm), lambda i: (i, 0, 0), memory_space=pltpu.SMEM),
            pl.BlockSpec((1, 1, TOP_K * tm), lambda i: (jnp.minimum(i + 1, nblk - 1), 0, 0),
                         memory_space=pltpu.SMEM),
            pl.BlockSpec(memory_space=pl.ANY),
            pl.BlockSpec((tm, TOP_K), lambda i: (i, 0)),
            pl.BlockSpec((tm, d), lambda i: (i, 0)),
            pl.BlockSpec((1, 2, 6, d), lambda i: (batch_of(i), 0, 0, 0)),
            pl.BlockSpec((1, d), lambda i: (0, 0)),
        ],
        out_specs=pl.BlockSpec((tm, d), lambda i: (i, 0)),
        out_shape=jax.ShapeDtypeStruct((rows, d), F32),
        scratch_shapes=[pltpu.VMEM((2, TOP_K * tm) + y2.shape[1:], F32), pltpu.SemaphoreType.DMA((2,))],
        compiler_params=_cparams(("arbitrary",)),
        name="moe_combine",
    )(pos3, pos3, y2, gates, x, mod, final_g.reshape(1, d))


def _route(logits, n_experts):
    t_tok = logits.shape[0]
    top_val, top_idx = lax.top_k(logits, TOP_K)
    gates = jax.nn.softmax(top_val, axis=-1)
    flat_e = top_idx.reshape(-1).astype(jnp.int32)
    n_assign = t_tok * TOP_K
    experts = jnp.arange(n_experts, dtype=jnp.int32)
    counts = jnp.sum((flat_e[:, None] == experts[None, :]).astype(jnp.int32), axis=0)
    padded = (counts + MOE_ROWS - 1) // MOE_ROWS * MOE_ROWS
    incl = (experts[None, :] <= experts[:, None]).astype(jnp.int32)
    cum_pad = jnp.sum(incl * padded[None, :], axis=1)
    pad_start = cum_pad - padded
    start = jnp.sum(incl * counts[None, :], axis=1) - counts
    order = jnp.argsort(flat_e).astype(jnp.int32)
    rank = jnp.argsort(order).astype(jnp.int32)
    pos = pad_start[flat_e] + rank - start[flat_e]
    n_blocks = n_assign // MOE_ROWS + n_experts
    block_start = jnp.arange(n_blocks, dtype=jnp.int32) * MOE_ROWS
    block_expert = jnp.clip(jnp.sum((cum_pad[None, :] <= block_start[:, None]).astype(jnp.int32), axis=1),
                            0, n_experts - 1)
    slot_e = jnp.repeat(block_expert, MOE_ROWS)
    within = jnp.arange(n_blocks * MOE_ROWS, dtype=jnp.int32) - pad_start[slot_e]
    src = jnp.clip(start[slot_e] + within, 0, n_assign - 1)
    slot_tok = jnp.where(within < counts[slot_e], order[src] // TOP_K, 0)
    n_valid = cum_pad[-1:] // MOE_ROWS
    blocks = jnp.arange(n_blocks, dtype=jnp.int32)
    is_valid = blocks < n_valid[0]
    starts = is_valid & ((blocks == 0) | (block_expert != jnp.roll(block_expert, 1)))
    run_id = jnp.sum((starts[None, :] & (blocks[None, :] <= blocks[:, None])).astype(jnp.int32), axis=1) - 1
    later_start = starts[None, :] & (blocks[None, :] > blocks[:, None])
    next_start = jnp.min(jnp.where(later_start, blocks[None, :], n_blocks), axis=1)
    next_expert = jnp.where(next_start < n_blocks, block_expert[jnp.minimum(next_start, n_blocks - 1)], -1)
    return gates, slot_tok, pos, (block_expert, n_valid, jnp.maximum(run_id, 0), next_expert)


def _moe_layer(x, norm_g, mod, router_w, router_b, w1_all, b1, w2_all, b2, final_g, *,
               layer, rows_per_batch, n_ctx, tm_rows, final_norm):
    rows, d = x.shape
    n_experts = router_w.shape[1]
    ff = w2_all.shape[2]
    rw_pad = jnp.pad(router_w, ((0, 0), (0, LANE - n_experts)))
    rb_pad = jnp.pad(router_b, (0, LANE - n_experts)).reshape(1, LANE)
    hf, logits = _ffn_pre(x, norm_g, mod, rw_pad, rb_pad, tm=tm_rows, rows_per_batch=rows_per_batch, n_ctx=n_ctx)
    gates, slot_tok, pos, routing = _route(logits[:, :n_experts], n_experts)
    xs = _gather_rows(hf, slot_tok, routing[1], bm=MOE_ROWS, out_dtype=BF16)
    b_glu = b1[:, 0::2].reshape(n_experts, 1, ff)
    b_lin = b1[:, 1::2].reshape(n_experts, 1, ff)
    act = _moe_up(xs, w1_all, layer, b_glu, b_lin, routing, tn=_pick(ff, (1024, 512, 256, 128)))
    y2 = _moe_down(act, w2_all, layer, b2.reshape(n_experts, 1, d), routing)
    tm_c = _pick(rows_per_batch, (128,))
    return _moe_combine(y2, pos, gates, x, mod, final_g, tm=tm_c, rows_per_batch=rows_per_batch,
                        n_ctx=n_ctx, final_norm=final_norm)


def _modulation(c, c_ctx, ada_w, ada_b_i, layer):
    bsz, d = c.shape
    rows = 16
    s = jnp.concatenate([jax.nn.silu(c), jax.nn.silu(c_ctx)[None], jnp.zeros((rows - bsz - 1, d), F32)], axis=0)
    n = ada_w.shape[2]
    m = _fused_matmul(s, ada_w.reshape(-1, n), name="adaln_modulation", m_rows=rows, k=d, tm=rows,
                      tn=_pick(n, (1024, 512, 256, 128)), out_dtype=F32, rows_per_batch=rows,
                      w_row_block=layer, epilogue="bias", bias=ada_b_i)
    lat = m[:bsz].reshape(bsz, 6, d)
    ctx = jnp.broadcast_to(m[bsz].reshape(1, 6, d), (bsz, 6, d))
    return jnp.stack([ctx, lat], axis=1)


HALO_ROWS = 8


def _dwconv_silu_kernel(x_ref, prev_ref, next_ref, w_ref, b_ref, o_ref, *, tc, n_ctx, nt):
    i = pl.program_id(1)
    row0 = i * tc
    has_prev = (row0 != 0) & (row0 != n_ctx)
    has_next = (row0 + tc != n_ctx) & (row0 + tc != nt)
    x = x_ref[0]
    prev_row = jnp.where(has_prev, prev_ref[0, HALO_ROWS - 1:HALO_ROWS, :], 0.0)
    next_row = jnp.where(has_next, next_ref[0, 0:1, :], 0.0)
    r = lax.broadcasted_iota(jnp.int32, (tc, 1), 0)
    prev = jnp.where(r == 0, prev_row, pltpu.roll(x, 1, axis=0))
    nxt = jnp.where(r == tc - 1, next_row, pltpu.roll(x, tc - 1, axis=0))
    out = b_ref[...] + prev * w_ref[0:1, :] + x * w_ref[1:2, :] + nxt * w_ref[2:3, :]
    o_ref[0] = jax.nn.silu(out)


def _dwconv_silu(proj3, col_block, conv_w, conv_b, n_ctx):
    bsz, nt, _ = proj3.shape
    k, width = conv_w.shape
    assert k == 3
    tc = _pick(n_ctx, (256, 128))
    assert nt % tc == 0 and tc % HALO_ROWS == 0
    per = tc // HALO_ROWS
    last = nt // HALO_ROWS - 1
    return pl.pallas_call(
        functools.partial(_dwconv_silu_kernel, tc=tc, n_ctx=n_ctx, nt=nt),
        grid=(bsz, nt // tc),
        in_specs=[
            pl.BlockSpec((1, tc, width), lambda b, i: (b, i, col_block)),
            pl.BlockSpec((1, HALO_ROWS, width), lambda b, i: (b, jnp.maximum(i * per - 1, 0), col_block)),
            pl.BlockSpec((1, HALO_ROWS, width), lambda b, i: (b, jnp.minimum((i + 1) * per, last), col_block)),
            pl.BlockSpec((k, width), lambda b, i: (0, 0)),
            pl.BlockSpec((1, width), lambda b, i: (0, 0)),
        ],
        out_specs=pl.BlockSpec((1, tc, width), lambda b, i: (b, i, 0)),
        out_shape=jax.ShapeDtypeStruct((bsz, nt, width), F32),
        compiler_params=_cparams(("arbitrary", "arbitrary")),
        name="dwconv_silu",
    )(proj3, proj3, proj3, conv_w, conv_b.reshape(1, width))


def _rope_tables(n_lat):
    rows = n_lat // GRID_W
    row = jnp.broadcast_to(jnp.arange(rows)[:, None], (rows, GRID_W)).reshape(-1)
    col = jnp.broadcast_to(jnp.arange(GRID_W)[None, :], (rows, GRID_W)).reshape(-1)
    freqs = ROPE_THETA ** (-jnp.arange(ROPE_PAIRS, dtype=F32) / ROPE_PAIRS)
    ang = jnp.stack([row[:, None] * freqs, col[:, None] * freqs], axis=1)
    cos, sin = jnp.cos(ang).astype(F32), jnp.sin(ang).astype(F32)
    c_tab = jnp.concatenate([cos[:, 0], cos[:, 0], cos[:, 1], cos[:, 1]], axis=-1)
    s_tab = jnp.concatenate([-sin[:, 0], sin[:, 0], -sin[:, 1], sin[:, 1]], axis=-1)
    return c_tab, s_tab


def _swap16(w):
    lane = jnp.arange(w.shape[-1])
    return jnp.take(w, jnp.where(lane % 32 < 16, lane + 16, lane - 16), axis=-1)


def kernel(x, c, ctx, c_ctx, mix_norm_g, ffn_norm_g, ada_w, ada_b, hyb_w_in, hyb_conv_w, hyb_conv_b, hyb_dt_bias, hyb_a_log, hyb_d_skip, hyb_ssd_norm_g, hyb_v_norm_g, hyb_w_s, hyb_b_s, hyb_w_out, mla_w_in, mla_q_norm_g, mla_kv_norm_g, mla_w_uq, mla_w_ukv, mla_w_o, router_w, router_b, exp_w1, exp_b1, exp_w2, exp_b2, final_norm_g):
    bsz, n_lat, d = x.shape
    n_ctx = ctx.shape[1]
    nt = n_ctx + n_lat
    rows = bsz * nt
    tm_s = _pick(nt, (768, 384, 256, 128))
    tm_l = _pick(n_lat, (1024, 512, 256, 128))

    stream = jnp.concatenate([ctx, x], axis=1).reshape(rows, d)

    mod0 = _modulation(c, c_ctx, ada_w, ada_b[0], 0)
    cuts = (SSD_WIDTH, SSD_WIDTH + XBC_WIDTH, SSD_WIDTH + XBC_WIDTH + 2 * SSD_HEADS)
    w_in = hyb_w_in[0]
    w_main = jnp.concatenate([w_in[:, :cuts[0]], w_in[:, cuts[2]:], w_in[:, cuts[0]:cuts[1]]], axis=1).astype(BF16)
    w_dt = jnp.pad(w_in[:, cuts[1]:cuts[2]], ((0, 0), (0, LANE - 2 * SSD_HEADS))).astype(BF16)
    pro = dict(prologue="norm_mod", g=mix_norm_g[0], mod=mod0, sh_idx=0, sc_idx=1,
               rows_per_batch=nt, n_ctx=n_ctx, m_rows=rows, k=d, tm=tm_s)
    proj = _fused_matmul(stream, w_main, name="hyb_in_proj", tn=_pick(w_main.shape[1], (1024, 512, 256, 128)), out_dtype=F32, **pro)
    dt_raw = _fused_matmul(stream, w_dt, name="hyb_dt_proj", tn=LANE, out_dtype=F32, **pro)

    xbc_off = SSD_WIDTH + 2 * GM_WIDTH
    assert xbc_off % XBC_WIDTH == 0
    xbc = _dwconv_silu(proj.reshape(bsz, nt, -1), xbc_off // XBC_WIDTH, hyb_conv_w[0], hyb_conv_b[0], n_ctx)
    dt_bias = jnp.pad(hyb_dt_bias[0].reshape(1, -1), ((0, 0), (0, LANE - 2 * SSD_HEADS)))
    a_log = jnp.pad(hyb_a_log[0].reshape(1, -1), ((0, 0), (0, LANE - 2 * SSD_HEADS)))
    dt3 = dt_raw.reshape(bsz, nt, LANE)
    n_ctx_chunks = n_ctx // SSD_CHUNK
    y_f = _ssd_scan(xbc, dt3, dt_bias, a_log, rev=False, n_ctx_chunks=n_ctx_chunks)
    y_b = _ssd_scan(xbc, dt3, dt_bias, a_log, rev=True, n_ctx_chunks=n_ctx_chunks)

    d_exp = jnp.repeat(hyb_d_skip[0], SSD_HEAD_DIM).reshape(1, SSD_WIDTH)
    y_cat = _mixer_merge(y_f.reshape(rows, SSD_WIDTH), y_b.reshape(rows, SSD_WIDTH),
                         xbc.reshape(rows, XBC_WIDTH), proj, d_exp,
                         hyb_ssd_norm_g[0].reshape(1, -1), hyb_v_norm_g[0].reshape(1, -1),
                         hyb_w_s[0].astype(BF16), hyb_b_s[0].T)
    stream = _fused_matmul(y_cat, hyb_w_out[0].astype(BF16), name="hyb_out_proj", m_rows=rows, k=SSD_WIDTH + GM_WIDTH, tm=tm_s,
                           tn=_pick(d, (1024, 512, 256, 128)), out_dtype=F32, rows_per_batch=nt, n_ctx=n_ctx,
                           epilogue="gated_resid", resid=stream, mod=mod0, gate_idx=2)
    stream = _moe_layer(stream, ffn_norm_g[0], mod0, router_w[0], router_b[0], exp_w1, exp_b1[0],
                        exp_w2, exp_b2[0], final_norm_g, layer=0, rows_per_batch=nt, n_ctx=n_ctx, tm_rows=tm_s,
                        final_norm=False)

    mod1 = _modulation(c, c_ctx, ada_w, ada_b[1], 1)
    w_in1 = mla_w_in[0]
    kpe_w = w_in1[:, Q_LORA + KV_LORA:]
    w_in1_r = jnp.concatenate([w_in1[:, Q_LORA:Q_LORA + KV_LORA], kpe_w, _swap16(kpe_w),
                               jnp.zeros((d, Q_LORA - KV_LORA - 2 * QK_ROPE), F32), w_in1[:, :Q_LORA]],
                              axis=1).astype(BF16)
    p1 = _fused_matmul(stream, w_in1_r, name="mla_in_proj", m_rows=rows, k=d, tm=tm_s, tn=w_in1_r.shape[1], out_dtype=F32,
                       prologue="norm_mod", g=mix_norm_g[1], mod=mod1, sh_idx=0, sc_idx=1,
                       rows_per_batch=nt, n_ctx=n_ctx)

    c_tab, s_tab = _rope_tables(n_lat)
    ck = jnp.concatenate([jnp.ones((n_ctx, QK_ROPE), F32), c_tab], axis=0)
    sk = jnp.concatenate([jnp.zeros((n_ctx, QK_ROPE), F32), s_tab], axis=0)
    p1_3 = p1.reshape(bsz, nt, -1)
    k_rot = p1_3[:, :, KV_LORA:KV_LORA + QK_ROPE] * ck + p1_3[:, :, KV_LORA + QK_ROPE:KV_LORA + 2 * QK_ROPE] * sk
    kr = jnp.concatenate([k_rot, k_rot], axis=-1).astype(BF16)

    kv = _fused_matmul(p1, mla_w_ukv[0].astype(BF16), name="mla_kv_up", m_rows=rows, k=KV_LORA, tm=tm_s,
                       tn=_pick(mla_w_ukv.shape[2], (2048, 1024, 512, 256, 128)), out_dtype=BF16,
                       rows_per_batch=nt, prologue="norm", g=mla_kv_norm_g[0], x_col_block=0)

    w_uq = mla_w_uq[0].reshape(Q_LORA, MLA_HEADS, QK_NOPE + QK_ROPE)
    w_q = jnp.concatenate([w_uq, _swap16(w_uq[:, :, QK_NOPE:])], axis=-1)
    w_q = w_q.reshape(Q_LORA, MLA_HEADS * (QK_NOPE + 2 * QK_ROPE)).astype(BF16)
    tm_q = _pick(n_ctx, (256, 128))
    lat_blocks = n_lat // tm_q
    q_rows = lambda i: (i // lat_blocks) * (nt // tm_q) + n_ctx // tm_q + i % lat_blocks
    q_tab = jnp.concatenate([jnp.ones((n_lat, QK_NOPE), F32), c_tab, s_tab], axis=-1) * (MLA_SCALE * math.log2(math.e))
    q = _fused_matmul(p1, w_q, name="mla_q_proj", m_rows=bsz * n_lat, k=Q_LORA, tm=tm_q, tn=w_q.shape[1],
                      out_dtype=BF16, rows_per_batch=n_lat, prologue="norm", g=mla_q_norm_g[0],
                      x_row_map=q_rows, x_col_block=1, epilogue="mul_table", table=q_tab,
                      table_row_map=lambda i: i % lat_blocks)

    o = _attention(q.reshape(bsz, n_lat, -1), kv.reshape(bsz, nt, -1), kr,
                   tq=_pick(n_lat, (1024, 512, 256, 128)), tk=_pick(nt, (2816, 768, 256, 128)))

    lat = stream.reshape(bsz, nt, d)[:, n_ctx:].reshape(bsz * n_lat, d)
    mod1_lat = mod1
    lat = _fused_matmul(o.reshape(bsz * n_lat, -1), mla_w_o[0].astype(BF16), name="mla_out_proj",
                        m_rows=bsz * n_lat,
                        k=MLA_HEADS * V_DIM, tm=tm_l, tn=_pick(d, (1024, 512, 256, 128)), out_dtype=F32,
                        rows_per_batch=n_lat, epilogue="gated_resid", resid=lat, mod=mod1_lat, gate_idx=2)
    out = _moe_layer(lat, ffn_norm_g[1], mod1_lat, router_w[1], router_b[1], exp_w1, exp_b1[1],
                     exp_w2, exp_b2[1], final_norm_g, layer=1, rows_per_batch=n_lat, n_ctx=0, tm_rows=tm_l,
                     final_norm=True)
    return out.reshape(bsz, n_lat, d)
```

```python
import functools
import math

import jax
import jax.numpy as jnp
from jax import lax
from jax.experimental import pallas as pl
from jax.experimental.pallas import tpu as pltpu

F32 = jnp.float32
BF16 = jnp.bfloat16
HIGHEST = lax.Precision.HIGHEST
EPS = 1e-6

GRID_W = 64
SSD_HEADS = 32
SSD_HEAD_DIM = 64
SSD_WIDTH = SSD_HEADS * SSD_HEAD_DIM
SSD_GROUPS = 4
SSD_HEADS_PER_GROUP = SSD_HEADS // SSD_GROUPS
SSD_STATE = 128
SSD_CHUNK = 128
SSD_GROUP_WIDTH = SSD_HEADS_PER_GROUP * SSD_HEAD_DIM
BC_WIDTH = SSD_GROUPS * SSD_STATE
XBC_WIDTH = SSD_WIDTH + 2 * BC_WIDTH
GM_GROUPS = 16
GM_GROUP_DIM = 128
GM_WIDTH = GM_GROUPS * GM_GROUP_DIM
GM_CHUNK = 128
MLA_HEADS = 16
Q_LORA = 768
KV_LORA = 512
QK_NOPE = 128
QK_ROPE = 64
V_DIM = 128
ROPE_PAIRS = QK_ROPE // 4
ROPE_THETA = 10000.0
MLA_SCALE = (QK_NOPE + QK_ROPE) ** -0.5
TOP_K = 4
SWIGLU_LIMIT = 7.0
SWIGLU_ALPHA = 1.702
MOE_ROWS = 256

LANE = 128
DMA_QUEUES = 2
VMEM_LIMIT = 56 * 1024 * 1024


def _pick(n, candidates):
    for c in candidates:
        if n % c == 0:
            return c
    raise ValueError(f"no tile for {n} in {candidates}")


def _cparams(sem):
    return pltpu.CompilerParams(dimension_semantics=sem, vmem_limit_bytes=VMEM_LIMIT)


def _is_ctx_rows(i, tm, rows_per_batch, n_ctx):
    row = (i * tm) % rows_per_batch + lax.broadcasted_iota(jnp.int32, (tm, 1), 0)
    return row < n_ctx


def _mod_select(mod_ref, idx, is_ctx):
    return jnp.where(is_ctx, mod_ref[0, 0, idx:idx + 1, :], mod_ref[0, 1, idx:idx + 1, :])


def _norm_rows(x, g):
    y = x * lax.rsqrt(jnp.mean(x * x, axis=-1, keepdims=True) + EPS)
    return y * g


def _fused_matmul_kernel(*refs, prologue, epilogue, tm, tn, rows_per_batch, n_ctx,
                         sh_idx, sc_idx, gate_idx, use_scratch):
    it = iter(refs)
    x_ref = next(it)
    g_ref = next(it) if prologue != "none" else None
    pmod_ref = next(it) if prologue == "norm_mod" else None
    w_ref = next(it)
    bias_ref = next(it) if epilogue == "bias" else None
    resid_ref = next(it) if epilogue == "gated_resid" else None
    emod_ref = next(it) if epilogue == "gated_resid" else None
    tab_ref = next(it) if epilogue == "mul_table" else None
    o_ref = next(it)
    xn_ref = next(it) if use_scratch else None

    i = pl.program_id(0)
    j = pl.program_id(1)

    if use_scratch:
        @pl.when(j == 0)
        def _():
            x = x_ref[...].astype(F32)
            if prologue != "none":
                x = _norm_rows(x, g_ref[...])
            if prologue == "norm_mod":
                is_ctx = _is_ctx_rows(i, tm, rows_per_batch, n_ctx)
                x = x * (1.0 + _mod_select(pmod_ref, sc_idx, is_ctx)) + _mod_select(pmod_ref, sh_idx, is_ctx)
            xn_ref[...] = x.astype(BF16)
        xn = xn_ref[...]
    else:
        xn = x_ref[...]

    acc = jnp.dot(xn, w_ref[...].astype(BF16), preferred_element_type=F32)
    if epilogue == "bias":
        acc = acc + bias_ref[...]
    elif epilogue == "gated_resid":
        is_ctx = _is_ctx_rows(i, tm, rows_per_batch, n_ctx)
        acc = resid_ref[...] + _mod_select(emod_ref, gate_idx, is_ctx) * acc
    elif epilogue == "mul_table":
        acc = acc * jnp.tile(tab_ref[...], (1, tn // tab_ref.shape[1]))
    o_ref[...] = acc.astype(o_ref.dtype)


def _fused_matmul(x, w, *, name, m_rows, k, tm, tn, out_dtype, rows_per_batch, n_ctx=0,
                  prologue="none", g=None, mod=None, sh_idx=0, sc_idx=0,
                  epilogue="none", bias=None, resid=None, gate_idx=0, table=None,
                  x_row_map=None, x_col_block=0, w_row_block=0, resid_row_map=None, table_row_map=None):
    n = w.shape[1]
    assert w.shape[0] % k == 0 and m_rows % tm == 0 and n % tn == 0 and rows_per_batch % tm == 0
    ident = lambda i: i
    x_row_map = x_row_map or ident
    resid_row_map = resid_row_map or ident
    table_row_map = table_row_map or ident
    batch_of = lambda i: (i * tm) // rows_per_batch
    use_scratch = prologue != "none" or x.dtype != BF16

    args = [x]
    in_specs = [pl.BlockSpec((tm, k), lambda i, j: (x_row_map(i), x_col_block))]
    if prologue != "none":
        args.append(g.reshape(1, k))
        in_specs.append(pl.BlockSpec((1, k), lambda i, j: (0, 0)))
    if prologue == "norm_mod":
        args.append(mod)
        in_specs.append(pl.BlockSpec((1, 2, 6, k), lambda i, j: (batch_of(i), 0, 0, 0)))
    args.append(w)
    in_specs.append(pl.BlockSpec((k, tn), lambda i, j: (w_row_block, j)))
    if epilogue == "bias":
        args.append(bias.reshape(1, n))
        in_specs.append(pl.BlockSpec((1, tn), lambda i, j: (0, j)))
    if epilogue == "gated_resid":
        args.append(resid)
        in_specs.append(pl.BlockSpec((tm, tn), lambda i, j: (resid_row_map(i), j)))
        args.append(mod)
        in_specs.append(pl.BlockSpec((1, 2, 6, tn), lambda i, j: (batch_of(i), 0, 0, j)))
    if epilogue == "mul_table":
        args.append(table)
        in_specs.append(pl.BlockSpec((tm, table.shape[1]), lambda i, j: (table_row_map(i), 0)))

    kern = functools.partial(
        _fused_matmul_kernel, prologue=prologue, epilogue=epilogue, tm=tm, tn=tn,
        rows_per_batch=rows_per_batch, n_ctx=n_ctx, sh_idx=sh_idx, sc_idx=sc_idx,
        gate_idx=gate_idx, use_scratch=use_scratch)
    return pl.pallas_call(
        kern,
        grid=(m_rows // tm, n // tn),
        in_specs=in_specs,
        out_specs=pl.BlockSpec((tm, tn), lambda i, j: (i, j)),
        out_shape=jax.ShapeDtypeStruct((m_rows, n), out_dtype),
        scratch_shapes=[pltpu.VMEM((tm, k), BF16)] if use_scratch else [],
        compiler_params=_cparams(("arbitrary", "arbitrary")),
        name=name,
    )(*args)


def _expand_heads(mat, col0):
    rows = mat.shape[0]
    return jnp.concatenate(
        [jnp.broadcast_to(mat[:, col0 + jj:col0 + jj + 1], (rows, SSD_HEAD_DIM))
         for jj in range(SSD_HEADS_PER_GROUP)], axis=1)


def _bf16_split3(mat):
    hi = mat.astype(BF16)
    r1 = mat - hi.astype(F32)
    mid = r1.astype(BF16)
    lo = (r1 - mid.astype(F32)).astype(BF16)
    return jnp.concatenate([hi, mid, lo], axis=1)


def _head_expansion(rev):
    k_i = lax.broadcasted_iota(jnp.int32, (SSD_GROUPS, 3 * LANE, SSD_GROUP_WIDTH), 1) % LANE
    n_i = lax.broadcasted_iota(jnp.int32, (SSD_GROUPS, 3 * LANE, SSD_GROUP_WIDTH), 2) // SSD_HEAD_DIM
    g_i = lax.broadcasted_iota(jnp.int32, (SSD_GROUPS, 3 * LANE, SSD_GROUP_WIDTH), 0)
    dcol = SSD_HEADS if rev else 0
    return (k_i == dcol + g_i * SSD_HEADS_PER_GROUP + n_i).astype(BF16)


def _ssd_bidir_kernel(xs_f, b_f, c_f, dtr_f, xs_b, b_b, c_b, dtr_b, dtb_ref, alog_ref, e_f, e_b,
                      y_f, y_b, h_f, h_b):
    @pl.when(pl.program_id(1) == 0)
    def _():
        h_f[...] = jnp.zeros_like(h_f)
        h_b[...] = jnp.zeros_like(h_b)

    _ssd_chunk(xs_f, b_f, c_f, dtr_f, dtb_ref, alog_ref, e_f, y_f, h_f, rev=False)
    _ssd_chunk(xs_b, b_b, c_b, dtr_b, dtb_ref, alog_ref, e_b, y_b, h_b, rev=True)


def _ssd_chunk(xs_ref, b_ref, c_ref, dtr_ref, dtb_ref, alog_ref, e_ref, y_ref, h_ref, *, rev):
    L = SSD_CHUNK
    dcol = SSD_HEADS if rev else 0
    dt = jax.nn.softplus(dtr_ref[0] + dtb_ref[...])
    a = dt * (-jnp.exp(alog_ref[...]))
    r_i = lax.broadcasted_iota(jnp.int32, (L, L), 0)
    c_i = lax.broadcasted_iota(jnp.int32, (L, L), 1)
    tri = (r_i <= c_i) if rev else (r_i >= c_i)
    acs = jnp.dot(tri.astype(F32), a, precision=HIGHEST, preferred_element_type=F32)
    acs_t = acs.T
    edge = acs[0:1, :] if rev else acs[L - 1:L, :]
    decay_end = jnp.exp(edge - acs)
    exp_acs = jnp.exp(acs)
    chunk_decay = jnp.exp(edge)
    dt3, exp_acs3, decay_end3 = _bf16_split3(dt), _bf16_split3(exp_acs), _bf16_split3(decay_end)
    spread = lambda pieces, g: jnp.dot(pieces, e_ref[g], preferred_element_type=F32)

    for g in range(SSD_GROUPS):
        col0 = dcol + g * SSD_HEADS_PER_GROUP
        bg = b_ref[0, :, g * SSD_STATE:(g + 1) * SSD_STATE]
        cg = c_ref[0, :, g * SSD_STATE:(g + 1) * SSD_STATE].astype(BF16)
        cb = lax.dot_general(cg, bg.astype(BF16), (((1,), (1,)), ((), ())), preferred_element_type=F32)
        xdt = xs_ref[0, :, g * SSD_GROUP_WIDTH:(g + 1) * SSD_GROUP_WIDTH] * spread(dt3, g)
        y_diag = []
        for jj in range(SSD_HEADS_PER_GROUP):
            col = col0 + jj
            seg = acs[:, col:col + 1] - acs_t[col:col + 1, :]
            lmat = jnp.exp(jnp.where(tri, seg, -jnp.inf))
            y_diag.append(jnp.dot((cb * lmat).astype(BF16),
                                  xdt[:, jj * SSD_HEAD_DIM:(jj + 1) * SSD_HEAD_DIM].astype(BF16),
                                  preferred_element_type=F32))
        y_diag = jnp.concatenate(y_diag, axis=1)
        h_g = h_ref[g]
        y_off = jnp.dot(cg, h_g.astype(BF16), preferred_element_type=F32) * spread(exp_acs3, g)
        y_ref[0, :, g * SSD_GROUP_WIDTH:(g + 1) * SSD_GROUP_WIDTH] = y_diag + y_off
        states = jnp.dot(bg.T.astype(BF16), (xdt * spread(decay_end3, g)).astype(BF16),
                         preferred_element_type=F32)
        h_ref[g] = h_g * _expand_heads(chunk_decay, col0) + states


def _ssd_scan(xbc, dt_raw, dt_bias, a_log, *, n_ctx_chunks):
    bsz, nt, _ = xbc.shape
    nc = nt // SSD_CHUNK
    fwd = lambda s: s
    bwd = lambda s: jnp.where(s < n_ctx_chunks, n_ctx_chunks - 1 - s, nc - 1 + n_ctx_chunks - s)
    nb_x = SSD_WIDTH // BC_WIDTH

    def chunk_specs(cmap):
        return [
            pl.BlockSpec((1, SSD_CHUNK, SSD_WIDTH), lambda b, s: (b, cmap(s), 0)),
            pl.BlockSpec((1, SSD_CHUNK, BC_WIDTH), lambda b, s: (b, cmap(s), nb_x)),
            pl.BlockSpec((1, SSD_CHUNK, BC_WIDTH), lambda b, s: (b, cmap(s), nb_x + 1)),
            pl.BlockSpec((1, SSD_CHUNK, LANE), lambda b, s: (b, cmap(s), 0)),
        ]

    const2 = pl.BlockSpec((1, LANE), lambda b, s: (0, 0))
    e_spec = pl.BlockSpec((SSD_GROUPS, 3 * LANE, SSD_GROUP_WIDTH), lambda b, s: (0, 0, 0))
    y_shape = jax.ShapeDtypeStruct((bsz, nt, SSD_WIDTH), F32)
    state = pltpu.VMEM((SSD_GROUPS, SSD_STATE, SSD_GROUP_WIDTH), F32)
    return pl.pallas_call(
        _ssd_bidir_kernel,
        grid=(bsz, nc),
        in_specs=chunk_specs(fwd) + chunk_specs(bwd) + [const2, const2, e_spec, e_spec],
        out_specs=[pl.BlockSpec((1, SSD_CHUNK, SSD_WIDTH), lambda b, s: (b, fwd(s), 0)),
                   pl.BlockSpec((1, SSD_CHUNK, SSD_WIDTH), lambda b, s: (b, bwd(s), 0))],
        out_shape=[y_shape, y_shape],
        scratch_shapes=[state, state],
        compiler_params=_cparams(("arbitrary", "arbitrary")),
        name="ssd_scan_bidir",
    )(xbc, xbc, xbc, dt_raw, xbc, xbc, xbc, dt_raw, dt_bias, a_log, _head_expansion(False), _head_expansion(True))


def _mixer_merge_kernel(yf_ref, yb_ref, xs_ref, z_ref, u_ref, v_ref, dsk_ref, sg_ref, vg_ref,
                        ws_ref, bst_ref, o_ref):
    y = yf_ref[...] + yb_ref[...] + xs_ref[...] * dsk_ref[...]
    y_ssd = _norm_rows(y * jax.nn.silu(z_ref[...]), sg_ref[...])
    o_ref[:, :SSD_WIDTH] = y_ssd.astype(o_ref.dtype)
    u = jax.nn.gelu(u_ref[...])
    v = _norm_rows(jax.nn.gelu(v_ref[...]), vg_ref[...]).astype(BF16)
    for g in range(GM_GROUPS):
        sl = slice(g * GM_GROUP_DIM, (g + 1) * GM_GROUP_DIM)
        s = jnp.dot(ws_ref[g], v[:, sl], preferred_element_type=F32) + bst_ref[:, g:g + 1]
        o_ref[:, SSD_WIDTH + g * GM_GROUP_DIM:SSD_WIDTH + (g + 1) * GM_GROUP_DIM] = (u[:, sl] * s).astype(o_ref.dtype)


def _mixer_merge(yf, yb, xbc, proj, d_exp, ssd_g, v_g, w_s, b_s_t):
    rows = yf.shape[0]
    tm = GM_CHUNK
    row = lambda c: (lambda i: (i, c))
    const2 = lambda i: (0, 0)
    return pl.pallas_call(
        _mixer_merge_kernel,
        grid=(rows // tm,),
        in_specs=[
            pl.BlockSpec((tm, SSD_WIDTH), row(0)),
            pl.BlockSpec((tm, SSD_WIDTH), row(0)),
            pl.BlockSpec((tm, SSD_WIDTH), row(0)),
            pl.BlockSpec((tm, SSD_WIDTH), row(0)),
            pl.BlockSpec((tm, GM_WIDTH), row(1)),
            pl.BlockSpec((tm, GM_WIDTH), row(2)),
            pl.BlockSpec((1, SSD_WIDTH), const2),
            pl.BlockSpec((1, SSD_WIDTH), const2),
            pl.BlockSpec((1, GM_WIDTH), const2),
            pl.BlockSpec((GM_GROUPS, GM_CHUNK, GM_CHUNK), lambda i: (0, 0, 0)),
            pl.BlockSpec((GM_CHUNK, GM_GROUPS), const2),
        ],
        out_specs=pl.BlockSpec((tm, SSD_WIDTH + GM_WIDTH), lambda i: (i, 0)),
        out_shape=jax.ShapeDtypeStruct((rows, SSD_WIDTH + GM_WIDTH), BF16),
        compiler_params=_cparams(("arbitrary",)),
        name="mixer_merge",
    )(yf, yb, xbc, proj, proj, proj, d_exp, ssd_g, v_g, w_s, b_s_t)


def _attn_kernel(q_ref, kn_ref, kr_ref, v_ref, o_ref, kcat_ref, *, tk, n_kv):
    @pl.when(pl.program_id(2) == 0)
    def _():
        kcat_ref[:, :QK_NOPE] = kn_ref[0]
        kcat_ref[:, QK_NOPE:] = kr_ref[0]

    q = q_ref[0]
    tq = q.shape[0]

    def body(c, carry):
        m, l, acc = carry
        off = pl.multiple_of(c * tk, tk)
        s = lax.dot_general(q, kcat_ref[pl.ds(off, tk), :], (((1,), (1,)), ((), ())),
                            preferred_element_type=F32)
        m_new = jnp.maximum(m, jnp.max(s, axis=-1, keepdims=True))
        alpha = jnp.exp2(m - m_new)
        p = jnp.exp2(s - m_new)
        l = alpha * l + jnp.sum(p, axis=-1, keepdims=True)
        acc = alpha * acc + jnp.dot(p.astype(BF16), v_ref[0, pl.ds(off, tk), :], preferred_element_type=F32)
        return m_new, l, acc

    init = (jnp.full((tq, 1), -jnp.inf, F32), jnp.zeros((tq, 1), F32), jnp.zeros((tq, V_DIM), F32))
    _, l, acc = lax.fori_loop(0, n_kv, body, init, unroll=True)
    o_ref[0] = (acc / l).astype(o_ref.dtype)


def _attention(q, kv, kr, *, tq, tk):
    bsz, nq, _ = q.shape
    nt = kv.shape[1]
    qk_w = QK_NOPE + 2 * QK_ROPE
    return pl.pallas_call(
        functools.partial(_attn_kernel, tk=tk, n_kv=nt // tk),
        grid=(bsz, MLA_HEADS, nq // tq),
        in_specs=[
            pl.BlockSpec((1, tq, qk_w), lambda b, h, i: (b, i, h)),
            pl.BlockSpec((1, nt, QK_NOPE), lambda b, h, i: (b, 0, 2 * h)),
            pl.BlockSpec((1, nt, 2 * QK_ROPE), lambda b, h, i: (b, 0, 0)),
            pl.BlockSpec((1, nt, V_DIM), lambda b, h, i: (b, 0, 2 * h + 1)),
        ],
        out_specs=pl.BlockSpec((1, tq, V_DIM), lambda b, h, i: (b, i, h)),
        out_shape=jax.ShapeDtypeStruct((bsz, nq, MLA_HEADS * V_DIM), BF16),
        scratch_shapes=[pltpu.VMEM((nt, qk_w), BF16)],
        compiler_params=_cparams(("arbitrary", "arbitrary", "arbitrary")),
        name="mla_attention",
    )(q, kv, kr, kv)


def _ffn_pre_kernel(x_ref, g_ref, mod_ref, rw_ref, rb_ref, h_ref, lg_ref, *, tm, rows_per_batch, n_ctx):
    i = pl.program_id(0)
    is_ctx = _is_ctx_rows(i, tm, rows_per_batch, n_ctx)
    h = _norm_rows(x_ref[...], g_ref[...])
    h = h * (1.0 + _mod_select(mod_ref, 4, is_ctx)) + _mod_select(mod_ref, 3, is_ctx)
    h_ref[...] = h.reshape(h_ref.shape).astype(h_ref.dtype)
    lg_ref[...] = jnp.dot(h, rw_ref[...], precision=HIGHEST, preferred_element_type=F32) + rb_ref[...]


def _ffn_pre(x, g, mod, rw_pad, rb_pad, *, tm, rows_per_batch, n_ctx):
    rows, d = x.shape
    batch_of = lambda i: (i * tm) // rows_per_batch
    return pl.pallas_call(
        functools.partial(_ffn_pre_kernel, tm=tm, rows_per_batch=rows_per_batch, n_ctx=n_ctx),
        grid=(rows // tm,),
        in_specs=[
            pl.BlockSpec((tm, d), lambda i: (i, 0)),
            pl.BlockSpec((1, d), lambda i: (0, 0)),
            pl.BlockSpec((1, 2, 6, d), lambda i: (batch_of(i), 0, 0, 0)),
            pl.BlockSpec((d, LANE), lambda i: (0, 0)),
            pl.BlockSpec((1, LANE), lambda i: (0, 0)),
        ],
        out_specs=[pl.BlockSpec((tm, d // LANE, LANE), lambda i: (i, 0, 0)),
                   pl.BlockSpec((tm, LANE), lambda i: (i, 0))],
        out_shape=[jax.ShapeDtypeStruct((rows, d // LANE, LANE), BF16), jax.ShapeDtypeStruct((rows, LANE), F32)],
        compiler_params=_cparams(("arbitrary",)),
        name="ffn_prenorm_router",
    )(x, g.reshape(1, d), mod, rw_pad, rb_pad)


def _row_copy(src_hbm, row, buf, slot, sem):
    return pltpu.make_async_copy(src_hbm.at[pl.ds(row, 1)], buf.at[pl.ds(slot, 1)], sem)


def _start_rows(src_hbm, ids_ref, n_rows, dst_row, buf, sem, row0=0):
    def body(h, c):
        for q in range(DMA_QUEUES):
            r = row0 + h * DMA_QUEUES + q
            _row_copy(src_hbm, ids_ref[0, 0, r], buf, dst_row(r), sem).start(priority=q)
        return c
    lax.fori_loop(0, n_rows // DMA_QUEUES, body, 0, unroll=4)


def _wait_rows(src_hbm, n_rows, buf, sem):
    def body(r, c):
        _row_copy(src_hbm, 0, buf, r, sem).wait()
        return c
    lax.fori_loop(0, n_rows, body, 0, unroll=8)


def _gather_rows_kernel(nv_ref, idx_ref, idx_next_ref, src_hbm, o_ref, buf, sem, *, bm):
    i = pl.program_id(0)
    nv = nv_ref[0]
    cur = i % 2
    same_row = lambda r: r

    @pl.when(i == 0)
    def _():
        _start_rows(src_hbm, idx_ref, bm, same_row, buf.at[0], sem.at[0])

    @pl.when(i + 1 < nv)
    def _():
        _start_rows(src_hbm, idx_next_ref, bm, same_row, buf.at[1 - cur], sem.at[1 - cur])

    @pl.when(i < nv)
    def _():
        _wait_rows(src_hbm, bm, buf.at[cur], sem.at[cur])
        o_ref[...] = buf[cur].astype(F32).reshape(o_ref.shape).astype(o_ref.dtype)

    @pl.when(i >= nv)
    def _():
        o_ref[...] = jnp.zeros_like(o_ref)


def _gather_rows(src, idx, n_valid, *, bm, out_dtype):
    n_slots = idx.shape[0]
    row_shape = src.shape[1:]
    d = row_shape[0] * row_shape[1]
    nb = n_slots // bm
    clamp = lambda i, nv: jnp.minimum(i, nv[0] - 1)
    idx3 = idx.reshape(nb, 1, bm)
    return pl.pallas_call(
        functools.partial(_gather_rows_kernel, bm=bm),
        grid_spec=pltpu.PrefetchScalarGridSpec(
            num_scalar_prefetch=1,
            grid=(nb,),
            in_specs=[
                pl.BlockSpec((1, 1, bm), lambda i, nv: (clamp(i, nv), 0, 0), memory_space=pltpu.SMEM),
                pl.BlockSpec((1, 1, bm), lambda i, nv: (clamp(i + 1, nv), 0, 0), memory_space=pltpu.SMEM),
                pl.BlockSpec(memory_space=pl.ANY),
            ],
            out_specs=pl.BlockSpec((bm, d), lambda i, nv: (i, 0)),
            scratch_shapes=[pltpu.VMEM((2, bm) + row_shape, src.dtype), pltpu.SemaphoreType.DMA((2,))],
        ),
        out_shape=jax.ShapeDtypeStruct((n_slots, d), out_dtype),
        compiler_params=_cparams(("arbitrary",)),
        name="moe_gather_rows",
    )(n_valid, idx3, idx3, src)


DEINT_CHUNK = 2 * LANE


def _deinterleave_perm():
    k_i = lax.broadcasted_iota(jnp.int32, (DEINT_CHUNK, DEINT_CHUNK), 0)
    n_i = lax.broadcasted_iota(jnp.int32, (DEINT_CHUNK, DEINT_CHUNK), 1)
    return (k_i == jnp.where(n_i < LANE, 2 * n_i, 2 * (n_i - LANE) + 1)).astype(BF16)


def _expert_changed(be_ref, i):
    return (i == 0) | (be_ref[i] != be_ref[jnp.maximum(i - 1, 0)])


def _run_weights(w_hbm, wbuf, wsem, be_ref, run_ref, nxt_ref, i, layer_off):
    width = wbuf.shape[2]
    col = pl.multiple_of(pl.program_id(0) * width, width)
    slot = run_ref[i] % 2

    def copy(expert, s):
        return pltpu.make_async_copy(w_hbm.at[layer_off + expert, :, pl.ds(col, width)], wbuf.at[s], wsem.at[s])

    @pl.when(i == 0)
    def _():
        copy(be_ref[i], slot).start()

    copy(be_ref[i], slot).wait()

    @pl.when(nxt_ref[i] >= 0)
    def _():
        copy(nxt_ref[i], 1 - slot).start(priority=1)

    return slot


def _moe_up_kernel(be_ref, nv_ref, run_ref, nxt_ref, x_ref, w_hbm, p_ref, bg_ref, bl_ref, o_ref,
                   wbuf, wsem, wg_ref, wl_ref, *, layer_off):
    i = pl.program_id(1)
    valid = i < nv_ref[0]

    @pl.when(valid & _expert_changed(be_ref, i))
    def _():
        slot = _run_weights(w_hbm, wbuf, wsem, be_ref, run_ref, nxt_ref, i, layer_off)
        perm = p_ref[...]
        for c in range(wbuf.shape[2] // DEINT_CHUNK):
            r = jnp.dot(wbuf[slot, :, c * DEINT_CHUNK:(c + 1) * DEINT_CHUNK].astype(BF16), perm,
                        preferred_element_type=F32)
            wg_ref[:, c * LANE:(c + 1) * LANE] = r[:, :LANE].astype(BF16)
            wl_ref[:, c * LANE:(c + 1) * LANE] = r[:, LANE:].astype(BF16)

    @pl.when(valid)
    def _():
        x = x_ref[...]
        a_g = jnp.dot(x, wg_ref[...], preferred_element_type=F32) + bg_ref[...]
        a_l = jnp.dot(x, wl_ref[...], preferred_element_type=F32) + bl_ref[...]
        glu = jnp.minimum(a_g, SWIGLU_LIMIT)
        lin = jnp.clip(a_l, -SWIGLU_LIMIT, SWIGLU_LIMIT)
        o_ref[...] = (glu * jax.nn.sigmoid(SWIGLU_ALPHA * glu) * (lin + 1.0)).astype(o_ref.dtype)

    @pl.when(jnp.logical_not(valid))
    def _():
        o_ref[...] = jnp.zeros_like(o_ref)


def _moe_up(xs, w1_all, layer, b_glu, b_lin, routing, *, tn):
    n_slots, d = xs.shape
    n_layers, n_e, _, f2 = w1_all.shape
    ff = f2 // 2
    nb = n_slots // MOE_ROWS
    blk = lambda i, nv: jnp.minimum(i, nv[0] - 1)
    bmap = lambda j, i, be, nv, run, nxt: (be[blk(i, nv)], 0, j)
    return pl.pallas_call(
        functools.partial(_moe_up_kernel, layer_off=layer * n_e),
        grid_spec=pltpu.PrefetchScalarGridSpec(
            num_scalar_prefetch=4,
            grid=(ff // tn, nb),
            in_specs=[
                pl.BlockSpec((MOE_ROWS, d), lambda j, i, be, nv, run, nxt: (blk(i, nv), 0)),
                pl.BlockSpec(memory_space=pl.ANY),
                pl.BlockSpec((DEINT_CHUNK, DEINT_CHUNK), lambda j, i, be, nv, run, nxt: (0, 0)),
                pl.BlockSpec((None, 1, tn), bmap),
                pl.BlockSpec((None, 1, tn), bmap),
            ],
            out_specs=pl.BlockSpec((MOE_ROWS, tn), lambda j, i, be, nv, run, nxt: (i, j)),
            scratch_shapes=[pltpu.VMEM((2, d, 2 * tn), F32), pltpu.SemaphoreType.DMA((2,)),
                            pltpu.VMEM((d, tn), BF16), pltpu.VMEM((d, tn), BF16)],
        ),
        out_shape=jax.ShapeDtypeStruct((n_slots, ff), BF16),
        compiler_params=_cparams(("arbitrary", "arbitrary")),
        name="moe_up",
    )(*routing, xs, w1_all.reshape(n_layers * n_e, d, f2), _deinterleave_perm(), b_glu, b_lin)


def _moe_down_kernel(be_ref, nv_ref, run_ref, nxt_ref, a_ref, w_hbm, b_ref, o_ref, wbuf, wsem, wb_ref, *,
                     layer_off):
    i = pl.program_id(1)
    valid = i < nv_ref[0]

    @pl.when(valid & _expert_changed(be_ref, i))
    def _():
        slot = _run_weights(w_hbm, wbuf, wsem, be_ref, run_ref, nxt_ref, i, layer_off)
        wb_ref[...] = wbuf[slot].astype(BF16)

    @pl.when(valid)
    def _():
        y = jnp.dot(a_ref[...], wb_ref[...], preferred_element_type=F32) + b_ref[...]
        o_ref[...] = y.reshape(o_ref.shape)

    @pl.when(jnp.logical_not(valid))
    def _():
        o_ref[...] = jnp.zeros_like(o_ref)


def _moe_down(act, w2_all, layer, b2, routing):
    n_slots, ff = act.shape
    n_layers, n_e, _, d = w2_all.shape
    nb = n_slots // MOE_ROWS
    blk = lambda i, nv: jnp.minimum(i, nv[0] - 1)
    return pl.pallas_call(
        functools.partial(_moe_down_kernel, layer_off=layer * n_e),
        grid_spec=pltpu.PrefetchScalarGridSpec(
            num_scalar_prefetch=4,
            grid=(1, nb),
            in_specs=[
                pl.BlockSpec((MOE_ROWS, ff), lambda j, i, be, nv, run, nxt: (blk(i, nv), 0)),
                pl.BlockSpec(memory_space=pl.ANY),
                pl.BlockSpec((None, 1, d), lambda j, i, be, nv, run, nxt: (be[blk(i, nv)], 0, 0)),
            ],
            out_specs=pl.BlockSpec((MOE_ROWS, d // LANE, LANE), lambda j, i, be, nv, run, nxt: (i, 0, 0)),
            scratch_shapes=[pltpu.VMEM((2, ff, d), F32), pltpu.SemaphoreType.DMA((2,)), pltpu.VMEM((ff, d), BF16)],
        ),
        out_shape=jax.ShapeDtypeStruct((n_slots, d // LANE, LANE), F32),
        compiler_params=_cparams(("arbitrary", "arbitrary")),
        name="moe_down",
    )(*routing, act, w2_all.reshape(n_layers * n_e, ff, d), b2)


def _moe_combine_kernel(pos_ref, pos_next_ref, y_hbm, gate_ref, x_ref, mod_ref, fg_ref, o_ref, buf, sem, *,
                        tm, rows_per_batch, n_ctx, final_norm):
    n_rows = TOP_K * tm
    i = pl.program_id(0)
    cur = i % 2
    choice_major = lambda r: (r % TOP_K) * tm + r // TOP_K

    @pl.when(i == 0)
    def _():
        _start_rows(y_hbm, pos_ref, n_rows, choice_major, buf.at[0], sem.at[0])

    _wait_rows(y_hbm, n_rows, buf.at[cur], sem.at[cur])

    gates = gate_ref[...]
    for kk in range(TOP_K):
        @pl.when(i + 1 < pl.num_programs(0))
        def _():
            _start_rows(y_hbm, pos_next_ref, tm, choice_major, buf.at[1 - cur], sem.at[1 - cur], row0=kk * tm)

        term = gates[:, kk:kk + 1] * buf[cur, kk * tm:(kk + 1) * tm].reshape(x_ref.shape)
        if kk == 0:
            o_ref[...] = term
        else:
            o_ref[...] += term
    is_ctx = _is_ctx_rows(i, tm, rows_per_batch, n_ctx)
    out = x_ref[...] + _mod_select(mod_ref, 5, is_ctx) * o_ref[...]
    if final_norm:
        out = _norm_rows(out, fg_ref[...])
    o_ref[...] = out


def _moe_combine(y2, pos, gates, x, mod, final_g, *, tm, rows_per_batch, n_ctx, final_norm):
    rows, d = x.shape
    nblk = rows // tm
    batch_of = lambda i: (i * tm) // rows_per_batch
    pos3 = pos.reshape(nblk, 1, TOP_K * tm)
    return pl.pallas_call(
        functools.partial(_moe_combine_kernel, tm=tm, rows_per_batch=rows_per_batch, n_ctx=n_ctx,
                          final_norm=final_norm),
        grid=(nblk,),
        in_specs=[
            pl.BlockSpec((1, 1, TOP_K * tm), lambda i: (i, 0, 0), memory_space=pltpu.SMEM),
            pl.BlockSpec((1, 1, TOP_K * tm), lambda i: (jnp.minimum(i + 1, nblk - 1), 0, 0),
                         memory_space=pltpu.SMEM),
            pl.BlockSpec(memory_space=pl.ANY),
            pl.BlockSpec((tm, TOP_K), lambda i: (i, 0)),
            pl.BlockSpec((tm, d), lambda i: (i, 0)),
            pl.BlockSpec((1, 2, 6, d), lambda i: (batch_of(i), 0, 0, 0)),
            pl.BlockSpec((1, d), lambda i: (0, 0)),
        ],
        out_specs=pl.BlockSpec((tm, d), lambda i: (i, 0)),
        out_shape=jax.ShapeDtypeStruct((rows, d), F32),
        scratch_shapes=[pltpu.VMEM((2, TOP_K * tm) + y2.shape[1:], F32), pltpu.SemaphoreType.DMA((2,))],
        compiler_params=_cparams(("arbitrary",)),
        name="moe_combine",
    )(pos3, pos3, y2, gates, x, mod, final_g.reshape(1, d))


def _route(logits, n_experts):
    t_tok = logits.shape[0]
    top_val, top_idx = lax.top_k(logits, TOP_K)
    gates = jax.nn.softmax(top_val, axis=-1)
    flat_e = top_idx.reshape(-1).astype(jnp.int32)
    n_assign = t_tok * TOP_K
    experts = jnp.arange(n_experts, dtype=jnp.int32)
    counts = jnp.sum((flat_e[:, None] == experts[None, :]).astype(jnp.int32), axis=0)
    padded = (counts + MOE_ROWS - 1) // MOE_ROWS * MOE_ROWS
    incl = (experts[None, :] <= experts[:, None]).astype(jnp.int32)
    cum_pad = jnp.sum(incl * padded[None, :], axis=1)
    pad_start = cum_pad - padded
    start = jnp.sum(incl * counts[None, :], axis=1) - counts
    order = jnp.argsort(flat_e).astype(jnp.int32)
    rank = jnp.argsort(order).astype(jnp.int32)
    pos = pad_start[flat_e] + rank - start[flat_e]
    n_blocks = n_assign // MOE_ROWS + n_experts
    block_start = jnp.arange(n_blocks, dtype=jnp.int32) * MOE_ROWS
    block_expert = jnp.clip(jnp.sum((cum_pad[None, :] <= block_start[:, None]).astype(jnp.int32), axis=1),
                            0, n_experts - 1)
    slot_e = jnp.repeat(block_expert, MOE_ROWS)
    within = jnp.arange(n_blocks * MOE_ROWS, dtype=jnp.int32) - pad_start[slot_e]
    src = jnp.clip(start[slot_e] + within, 0, n_assign - 1)
    slot_tok = jnp.where(within < counts[slot_e], order[src] // TOP_K, 0)
    n_valid = cum_pad[-1:] // MOE_ROWS
    blocks = jnp.arange(n_blocks, dtype=jnp.int32)
    is_valid = blocks < n_valid[0]
    starts = is_valid & ((blocks == 0) | (block_expert != jnp.roll(block_expert, 1)))
    run_id = jnp.sum((starts[None, :] & (blocks[None, :] <= blocks[:, None])).astype(jnp.int32), axis=1) - 1
    later_start = starts[None, :] & (blocks[None, :] > blocks[:, None])
    next_start = jnp.min(jnp.where(later_start, blocks[None, :], n_blocks), axis=1)
    next_expert = jnp.where(next_start < n_blocks, block_expert[jnp.minimum(next_start, n_blocks - 1)], -1)
    return gates, slot_tok, pos, (block_expert, n_valid, jnp.maximum(run_id, 0), next_expert)


def _moe_layer(x, norm_g, mod, router_w, router_b, w1_all, b1, w2_all, b2, final_g, *,
               layer, rows_per_batch, n_ctx, tm_rows, final_norm):
    rows, d = x.shape
    n_experts = router_w.shape[1]
    ff = w2_all.shape[2]
    rw_pad = jnp.pad(router_w, ((0, 0), (0, LANE - n_experts)))
    rb_pad = jnp.pad(router_b, (0, LANE - n_experts)).reshape(1, LANE)
    hf, logits = _ffn_pre(x, norm_g, mod, rw_pad, rb_pad, tm=tm_rows, rows_per_batch=rows_per_batch, n_ctx=n_ctx)
    gates, slot_tok, pos, routing = _route(logits[:, :n_experts], n_experts)
    xs = _gather_rows(hf, slot_tok, routing[1], bm=MOE_ROWS, out_dtype=BF16)
    b_glu = b1[:, 0::2].reshape(n_experts, 1, ff)
    b_lin = b1[:, 1::2].reshape(n_experts, 1, ff)
    act = _moe_up(xs, w1_all, layer, b_glu, b_lin, routing, tn=_pick(ff, (1024, 512, 256, 128)))
    y2 = _moe_down(act, w2_all, layer, b2.reshape(n_experts, 1, d), routing)
    tm_c = _pick(rows_per_batch, (128,))
    return _moe_combine(y2, pos, gates, x, mod, final_g, tm=tm_c, rows_per_batch=rows_per_batch,
                        n_ctx=n_ctx, final_norm=final_norm)


def _modulation(c, c_ctx, ada_w, ada_b_i, layer):
    bsz, d = c.shape
    rows = 16
    s = jnp.concatenate([jax.nn.silu(c), jax.nn.silu(c_ctx)[None], jnp.zeros((rows - bsz - 1, d), F32)], axis=0)
    n = ada_w.shape[2]
    m = _fused_matmul(s, ada_w.reshape(-1, n), name="adaln_modulation", m_rows=rows, k=d, tm=rows,
                      tn=_pick(n, (1024, 512, 256, 128)), out_dtype=F32, rows_per_batch=rows,
                      w_row_block=layer, epilogue="bias", bias=ada_b_i)
    lat = m[:bsz].reshape(bsz, 6, d)
    ctx = jnp.broadcast_to(m[bsz].reshape(1, 6, d), (bsz, 6, d))
    return jnp.stack([ctx, lat], axis=1)


HALO_ROWS = 8


def _dwconv_silu_kernel(x_ref, prev_ref, next_ref, w_ref, b_ref, o_ref, *, tc, n_ctx, nt):
    i = pl.program_id(1)
    row0 = i * tc
    has_prev = (row0 != 0) & (row0 != n_ctx)
    has_next = (row0 + tc != n_ctx) & (row0 + tc != nt)
    x = x_ref[0]
    prev_row = jnp.where(has_prev, prev_ref[0, HALO_ROWS - 1:HALO_ROWS, :], 0.0)
    next_row = jnp.where(has_next, next_ref[0, 0:1, :], 0.0)
    r = lax.broadcasted_iota(jnp.int32, (tc, 1), 0)
    prev = jnp.where(r == 0, prev_row, pltpu.roll(x, 1, axis=0))
    nxt = jnp.where(r == tc - 1, next_row, pltpu.roll(x, tc - 1, axis=0))
    out = b_ref[...] + prev * w_ref[0:1, :] + x * w_ref[1:2, :] + nxt * w_ref[2:3, :]
    o_ref[0] = jax.nn.silu(out)


def _dwconv_silu(proj3, col_block, conv_w, conv_b, n_ctx):
    bsz, nt, _ = proj3.shape
    k, width = conv_w.shape
    assert k == 3
    tc = _pick(n_ctx, (256, 128))
    assert nt % tc == 0 and tc % HALO_ROWS == 0
    per = tc // HALO_ROWS
    last = nt // HALO_ROWS - 1
    return pl.pallas_call(
        functools.partial(_dwconv_silu_kernel, tc=tc, n_ctx=n_ctx, nt=nt),
        grid=(bsz, nt // tc),
        in_specs=[
            pl.BlockSpec((1, tc, width), lambda b, i: (b, i, col_block)),
            pl.BlockSpec((1, HALO_ROWS, width), lambda b, i: (b, jnp.maximum(i * per - 1, 0), col_block)),
            pl.BlockSpec((1, HALO_ROWS, width), lambda b, i: (b, jnp.minimum((i + 1) * per, last), col_block)),
            pl.BlockSpec((k, width), lambda b, i: (0, 0)),
            pl.BlockSpec((1, width), lambda b, i: (0, 0)),
        ],
        out_specs=pl.BlockSpec((1, tc, width), lambda b, i: (b, i, 0)),
        out_shape=jax.ShapeDtypeStruct((bsz, nt, width), F32),
        compiler_params=_cparams(("arbitrary", "arbitrary")),
        name="dwconv_silu",
    )(proj3, proj3, proj3, conv_w, conv_b.reshape(1, width))


def _rope_tables(n_lat):
    rows = n_lat // GRID_W
    row = jnp.broadcast_to(jnp.arange(rows)[:, None], (rows, GRID_W)).reshape(-1)
    col = jnp.broadcast_to(jnp.arange(GRID_W)[None, :], (rows, GRID_W)).reshape(-1)
    freqs = ROPE_THETA ** (-jnp.arange(ROPE_PAIRS, dtype=F32) / ROPE_PAIRS)
    ang = jnp.stack([row[:, None] * freqs, col[:, None] * freqs], axis=1)
    cos, sin = jnp.cos(ang).astype(F32), jnp.sin(ang).astype(F32)
    c_tab = jnp.concatenate([cos[:, 0], cos[:, 0], cos[:, 1], cos[:, 1]], axis=-1)
    s_tab = jnp.concatenate([-sin[:, 0], sin[:, 0], -sin[:, 1], sin[:, 1]], axis=-1)
    return c_tab, s_tab


def _swap16(w):
    lane = jnp.arange(w.shape[-1])
    return jnp.take(w, jnp.where(lane % 32 < 16, lane + 16, lane - 16), axis=-1)


def kernel(x, c, ctx, c_ctx, mix_norm_g, ffn_norm_g, ada_w, ada_b, hyb_w_in, hyb_conv_w, hyb_conv_b, hyb_dt_bias, hyb_a_log, hyb_d_skip, hyb_ssd_norm_g, hyb_v_norm_g, hyb_w_s, hyb_b_s, hyb_w_out, mla_w_in, mla_q_norm_g, mla_kv_norm_g, mla_w_uq, mla_w_ukv, mla_w_o, router_w, router_b, exp_w1, exp_b1, exp_w2, exp_b2, final_norm_g):
    bsz, n_lat, d = x.shape
    n_ctx = ctx.shape[1]
    nt = n_ctx + n_lat
    rows = bsz * nt
    tm_s = _pick(nt, (768, 384, 256, 128))
    tm_l = _pick(n_lat, (1024, 512, 256, 128))

    stream = jnp.concatenate([ctx, x], axis=1).reshape(rows, d)

    mod0 = _modulation(c, c_ctx, ada_w, ada_b[0], 0)
    cuts = (SSD_WIDTH, SSD_WIDTH + XBC_WIDTH, SSD_WIDTH + XBC_WIDTH + 2 * SSD_HEADS)
    w_in = hyb_w_in[0]
    w_main = jnp.concatenate([w_in[:, :cuts[0]], w_in[:, cuts[2]:], w_in[:, cuts[0]:cuts[1]]], axis=1).astype(BF16)
    w_dt = jnp.pad(w_in[:, cuts[1]:cuts[2]], ((0, 0), (0, LANE - 2 * SSD_HEADS))).astype(BF16)
    pro = dict(prologue="norm_mod", g=mix_norm_g[0], mod=mod0, sh_idx=0, sc_idx=1,
               rows_per_batch=nt, n_ctx=n_ctx, m_rows=rows, k=d, tm=tm_s)
    proj = _fused_matmul(stream, w_main, name="hyb_in_proj", tn=_pick(w_main.shape[1], (1024, 512, 256, 128)), out_dtype=F32, **pro)
    dt_raw = _fused_matmul(stream, w_dt, name="hyb_dt_proj", tn=LANE, out_dtype=F32, **pro)

    xbc_off = SSD_WIDTH + 2 * GM_WIDTH
    assert xbc_off % XBC_WIDTH == 0
    xbc = _dwconv_silu(proj.reshape(bsz, nt, -1), xbc_off // XBC_WIDTH, hyb_conv_w[0], hyb_conv_b[0], n_ctx)
    dt_bias = jnp.pad(hyb_dt_bias[0].reshape(1, -1), ((0, 0), (0, LANE - 2 * SSD_HEADS)))
    a_log = jnp.pad(hyb_a_log[0].reshape(1, -1), ((0, 0), (0, LANE - 2 * SSD_HEADS)))
    dt3 = dt_raw.reshape(bsz, nt, LANE)
    n_ctx_chunks = n_ctx // SSD_CHUNK
    y_f, y_b = _ssd_scan(xbc, dt3, dt_bias, a_log, n_ctx_chunks=n_ctx_chunks)

    d_exp = jnp.repeat(hyb_d_skip[0], SSD_HEAD_DIM).reshape(1, SSD_WIDTH)
    y_cat = _mixer_merge(y_f.reshape(rows, SSD_WIDTH), y_b.reshape(rows, SSD_WIDTH),
                         xbc.reshape(rows, XBC_WIDTH), proj, d_exp,
                         hyb_ssd_norm_g[0].reshape(1, -1), hyb_v_norm_g[0].reshape(1, -1),
                         hyb_w_s[0].astype(BF16), hyb_b_s[0].T)
    stream = _fused_matmul(y_cat, hyb_w_out[0].astype(BF16), name="hyb_out_proj", m_rows=rows, k=SSD_WIDTH + GM_WIDTH, tm=tm_s,
                           tn=_pick(d, (1024, 512, 256, 128)), out_dtype=F32, rows_per_batch=nt, n_ctx=n_ctx,
                           epilogue="gated_resid", resid=stream, mod=mod0, gate_idx=2)
    stream = _moe_layer(stream, ffn_norm_g[0], mod0, router_w[0], router_b[0], exp_w1, exp_b1[0],
                        exp_w2, exp_b2[0], final_norm_g, layer=0, rows_per_batch=nt, n_ctx=n_ctx, tm_rows=tm_s,
                        final_norm=False)

    mod1 = _modulation(c, c_ctx, ada_w, ada_b[1], 1)
    w_in1 = mla_w_in[0]
    kpe_w = w_in1[:, Q_LORA + KV_LORA:]
    w_in1_r = jnp.concatenate([w_in1[:, Q_LORA:Q_LORA + KV_LORA], kpe_w, _swap16(kpe_w),
                               jnp.zeros((d, Q_LORA - KV_LORA - 2 * QK_ROPE), F32), w_in1[:, :Q_LORA]],
                              axis=1).astype(BF16)
    p1 = _fused_matmul(stream, w_in1_r, name="mla_in_proj", m_rows=rows, k=d, tm=tm_s, tn=w_in1_r.shape[1], out_dtype=F32,
                       prologue="norm_mod", g=mix_norm_g[1], mod=mod1, sh_idx=0, sc_idx=1,
                       rows_per_batch=nt, n_ctx=n_ctx)

    c_tab, s_tab = _rope_tables(n_lat)
    ck = jnp.concatenate([jnp.ones((n_ctx, QK_ROPE), F32), c_tab], axis=0)
    sk = jnp.concatenate([jnp.zeros((n_ctx, QK_ROPE), F32), s_tab], axis=0)
    p1_3 = p1.reshape(bsz, nt, -1)
    k_rot = p1_3[:, :, KV_LORA:KV_LORA + QK_ROPE] * ck + p1_3[:, :, KV_LORA + QK_ROPE:KV_LORA + 2 * QK_ROPE] * sk
    kr = jnp.concatenate([k_rot, k_rot], axis=-1).astype(BF16)

    kv = _fused_matmul(p1, mla_w_ukv[0].astype(BF16), name="mla_kv_up", m_rows=rows, k=KV_LORA, tm=tm_s,
                       tn=_pick(mla_w_ukv.shape[2], (2048, 1024, 512, 256, 128)), out_dtype=BF16,
                       rows_per_batch=nt, prologue="norm", g=mla_kv_norm_g[0], x_col_block=0)

    w_uq = mla_w_uq[0].reshape(Q_LORA, MLA_HEADS, QK_NOPE + QK_ROPE)
    w_q = jnp.concatenate([w_uq, _swap16(w_uq[:, :, QK_NOPE:])], axis=-1)
    w_q = w_q.reshape(Q_LORA, MLA_HEADS * (QK_NOPE + 2 * QK_ROPE)).astype(BF16)
    tm_q = _pick(n_ctx, (256, 128))
    lat_blocks = n_lat // tm_q
    q_rows = lambda i: (i // lat_blocks) * (nt // tm_q) + n_ctx // tm_q + i % lat_blocks
    q_tab = jnp.concatenate([jnp.ones((n_lat, QK_NOPE), F32), c_tab, s_tab], axis=-1) * (MLA_SCALE * math.log2(math.e))
    q = _fused_matmul(p1, w_q, name="mla_q_proj", m_rows=bsz * n_lat, k=Q_LORA, tm=tm_q, tn=w_q.shape[1],
                      out_dtype=BF16, rows_per_batch=n_lat, prologue="norm", g=mla_q_norm_g[0],
                      x_row_map=q_rows, x_col_block=1, epilogue="mul_table", table=q_tab,
                      table_row_map=lambda i: i % lat_blocks)

    o = _attention(q.reshape(bsz, n_lat, -1), kv.reshape(bsz, nt, -1), kr,
                   tq=_pick(n_lat, (1024, 512, 256, 128)), tk=_pick(nt, (2816, 768, 256, 128)))

    lat = stream.reshape(bsz, nt, d)[:, n_ctx:].reshape(bsz * n_lat, d)
    mod1_lat = mod1
    lat = _fused_matmul(o.reshape(bsz * n_lat, -1), mla_w_o[0].astype(BF16), name="mla_out_proj",
                        m_rows=bsz * n_lat,
                        k=MLA_HEADS * V_DIM, tm=tm_l, tn=_pick(d, (1024, 512, 256, 128)), out_dtype=F32,
                        rows_per_batch=n_lat, epilogue="gated_resid", resid=lat, mod=mod1_lat, gate_idx=2)
    out = _moe_layer(lat, ffn_norm_g[1], mod1_lat, router_w[1], router_b[1], exp_w1, exp_b1[1],
                     exp_w2, exp_b2[1], final_norm_g, layer=1, rows_per_batch=n_lat, n_ctx=0, tm_rows=tm_l,
                     final_norm=True)
    return out.reshape(bsz, n_lat, d)
```
